```python
import jax, jax.numpy as jnp
from jax import lax
import numpy as np

D_MODEL = 1024
BATCH = 16
SEQ = 4096
DEPTH = 4

GRID_W = 64
CTX_LEN = 256
CHUNK = 128
DA = 2 * D_MODEL
GA = 16
HEAD_DIM = 64
N_HEADS = D_MODEL // HEAD_DIM
N_KV_HEADS = N_HEADS // 4
Q_PER_KV = N_HEADS // N_KV_HEADS
WINDOW = 128
BLOCK = 128
ROPE_THETA = 10000.0
ROPE_PAIRS_PER_AXIS = HEAD_DIM // 4
N_EXPERTS = 16
CAPACITY_FACTOR = 2
D_FF_EXPERT = 2 * D_MODEL
EPS = 1e-6
N_A_LAYERS = (DEPTH + 1) // 2
N_B_LAYERS = DEPTH // 2

kernel_name = "hybrid_gmlp_swa_ecmoe_dit"


def rmsnorm(x, g):
    xf = x.astype(jnp.float32)
    y = xf * lax.rsqrt(jnp.mean(xf * xf, axis=-1, keepdims=True) + EPS)
    return (y * g.astype(jnp.float32)).astype(x.dtype)


def layernorm(x, g, b):
    xf = x.astype(jnp.float32)
    mu = jnp.mean(xf, axis=-1, keepdims=True)
    var = jnp.mean(jnp.square(xf - mu), axis=-1, keepdims=True)
    y = (xf - mu) * lax.rsqrt(var + EPS)
    return (y * g.astype(jnp.float32) + b.astype(jnp.float32)).astype(x.dtype)


def axial_rope(x, cos, sin):
    half = x.shape[-1] // 2
    shp = (x.shape[1],) + (1,) * (x.ndim - 3) + (half,)
    cs = cos.reshape(shp)
    sn = sin.reshape(shp)
    x1, x2 = x[..., :half], x[..., half:]
    return jnp.concatenate([x1 * cs - x2 * sn, x1 * sn + x2 * cs], axis=-1)


def chunk_gmlp(h, w_in, b_in, ln_g, ln_b, w_s, b_s, w_out):
    B, N, _ = h.shape
    z = jax.nn.gelu(h @ w_in + b_in)
    u, v = z[..., :DA], z[..., DA:]
    v = layernorm(v, ln_g, ln_b)
    vc = v.reshape(B, N // CHUNK, CHUNK, GA, DA // GA)
    sv = jnp.einsum('gpq,bnqgc->bnpgc', w_s, vc) + jnp.swapaxes(b_s, 0, 1)[None, None, :, :, None]
    return (u * sv.reshape(B, N, DA)) @ w_out


def _project_qkv(h, w_qkv):
    B, N, _ = h.shape
    qkv = h @ w_qkv
    nq = N_HEADS * HEAD_DIM
    nk = N_KV_HEADS * HEAD_DIM
    q = qkv[..., :nq].reshape(B, N, N_KV_HEADS, Q_PER_KV, HEAD_DIM)
    k = qkv[..., nq:nq + nk].reshape(B, N, N_KV_HEADS, HEAD_DIM)
    v = qkv[..., nq + nk:].reshape(B, N, N_KV_HEADS, HEAD_DIM)
    return q, k, v


def windowed_gqa(h_lat, h_ctx, w_qkv, sink, w_o, cos, sin, need_ctx_out):
    B, L, _ = h_lat.shape
    Lc = h_ctx.shape[1]
    dt = h_lat.dtype
    scale = HEAD_DIM ** -0.5
    q_l, k_l, v_l = _project_qkv(h_lat, w_qkv)
    q_c, k_c, v_c = _project_qkv(h_ctx, w_qkv)
    q_l = axial_rope(q_l, cos, sin) * scale
    k_l = axial_rope(k_l, cos, sin)
    sink_f = sink.astype(jnp.float32).reshape(N_KV_HEADS, Q_PER_KV)

    nb = L // BLOCK
    qb = q_l.reshape(B, nb, BLOCK, N_KV_HEADS, Q_PER_KV, HEAD_DIM)
    pad = ((0, 0), (BLOCK, BLOCK), (0, 0), (0, 0))
    kp = jnp.pad(k_l, pad).reshape(B, nb + 2, BLOCK, N_KV_HEADS, HEAD_DIM)
    vp = jnp.pad(v_l, pad).reshape(B, nb + 2, BLOCK, N_KV_HEADS, HEAD_DIM)
    kw = jnp.concatenate([kp[:, :-2], kp[:, 1:-1], kp[:, 2:]], axis=2)
    vw = jnp.concatenate([vp[:, :-2], vp[:, 1:-1], vp[:, 2:]], axis=2)
    iq = jnp.arange(BLOCK)[:, None]
    jk = jnp.arange(3 * BLOCK)[None, :]
    band = jnp.abs(jk - BLOCK - iq) <= WINDOW
    pos = jnp.arange(nb)[:, None] * BLOCK - BLOCK + jnp.arange(3 * BLOCK)[None, :]
    valid = (pos >= 0) & (pos < L)
    mask = (band[None] & valid[:, None, :])[None, :, None, None]
    s_w = jnp.einsum('bnqkgd,bnskd->bnkgqs', qb, kw).astype(jnp.float32)
    s_w = jnp.where(mask, s_w, -jnp.inf)
    s_c = jnp.einsum('bnqkgd,bskd->bnkgqs', qb, k_c).astype(jnp.float32)
    sk = sink_f[None, None, :, :, None, None]
    m = jnp.maximum(jnp.maximum(s_w.max(-1, keepdims=True), s_c.max(-1, keepdims=True)), sk)
    p_w = jnp.exp(s_w - m)
    p_c = jnp.exp(s_c - m)
    den = p_w.sum(-1, keepdims=True) + p_c.sum(-1, keepdims=True) + jnp.exp(sk - m)
    o = (jnp.einsum('bnkgqs,bnskd->bnqkgd', (p_w / den).astype(dt), vw)
         + jnp.einsum('bnkgqs,bskd->bnqkgd', (p_c / den).astype(dt), v_c))
    o_lat = o.reshape(B, L, N_HEADS * HEAD_DIM) @ w_o

    if not need_ctx_out:
        return o_lat, None
    s = jnp.einsum('bqkgd,bskd->bkgqs', q_c * scale, k_c).astype(jnp.float32)
    skc = sink_f[None, :, :, None, None]
    mc = jnp.maximum(s.max(-1, keepdims=True), skc)
    pc = jnp.exp(s - mc)
    pc = pc / (pc.sum(-1, keepdims=True) + jnp.exp(skc - mc))
    oc = jnp.einsum('bkgqs,bskd->bqkgd', pc.astype(dt), v_c)
    o_ctx = oc.reshape(B, Lc, N_HEADS * HEAD_DIM) @ w_o
    return o_lat, o_ctx


def expert_choice_ffn(h, w_r, w1, w3, w2):
    B, N, D = h.shape
    cap = (CAPACITY_FACTOR * N) // N_EXPERTS
    aff = jax.nn.softmax((h @ w_r).astype(jnp.float32), axis=-1)
    gates, idx = lax.top_k(jnp.swapaxes(aff, 1, 2), cap)
    xg = jax.vmap(lambda hb, ib: hb[ib])(h, idx)
    a = jnp.einsum('becd,edf->becf', xg, w1)
    b = jnp.einsum('becd,edf->becf', xg, w3)
    y = jnp.einsum('becf,efd->becd', jax.nn.silu(a) * b, w2) * gates[..., None].astype(h.dtype)
    return jax.vmap(lambda ib, yb: jnp.zeros((N, D), yb.dtype).at[ib.reshape(-1)].add(yb.reshape(-1, D)))(idx, y)


def setup_inputs(seed: int = 0) -> dict:
    key = jax.random.key(seed)
    ks = jax.random.split(key, 24)
    f32 = jnp.float32

    def nrm(k, shape, scale):
        return jax.random.normal(k, shape, f32) * scale

    D = D_MODEL
    qkv_w = (N_HEADS + 2 * N_KV_HEADS) * HEAD_DIM
    return {
        "x": nrm(ks[0], (BATCH, SEQ, D), 1.0),
        "c": nrm(ks[1], (BATCH, D), 1.0),
        "ctx": nrm(ks[2], (BATCH, CTX_LEN, D), 1.0),
        "c_ctx": nrm(ks[3], (D,), 1.0),
        "w_mod": nrm(ks[4], (DEPTH, D, 6 * D), 0.5 * D ** -0.5),
        "b_mod": nrm(ks[5], (DEPTH, 6 * D), 0.02),
        "g_norm_mix": 1.0 + nrm(ks[6], (DEPTH, D), 0.02),
        "g_norm_ffn": 1.0 + nrm(ks[7], (DEPTH, D), 0.02),
        "a_w_in": nrm(ks[8], (N_A_LAYERS, D, 2 * DA), D ** -0.5),
        "a_b_in": nrm(ks[9], (N_A_LAYERS, 2 * DA), 0.02),
        "a_ln_g": 1.0 + nrm(ks[10], (N_A_LAYERS, DA), 0.02),
        "a_ln_b": nrm(ks[11], (N_A_LAYERS, DA), 0.02),
        "a_w_s": nrm(ks[12], (N_A_LAYERS, GA, CHUNK, CHUNK), CHUNK ** -0.5),
        "a_b_s": 1.0 + nrm(ks[13], (N_A_LAYERS, GA, CHUNK), 0.02),
        "a_w_out": nrm(ks[14], (N_A_LAYERS, DA, D), DA ** -0.5),
        "b_w_qkv": nrm(ks[15], (N_B_LAYERS, D, qkv_w), D ** -0.5),
        "b_sink": nrm(ks[16], (N_B_LAYERS, N_HEADS), 0.5),
        "b_w_o": nrm(ks[17], (N_B_LAYERS, N_HEADS * HEAD_DIM, D), (N_HEADS * HEAD_DIM) ** -0.5),
        "r_w": nrm(ks[18], (DEPTH, D, N_EXPERTS), D ** -0.5),
        "e_w1": nrm(ks[19], (DEPTH, N_EXPERTS, D, D_FF_EXPERT), D ** -0.5),
        "e_w3": nrm(ks[20], (DEPTH, N_EXPERTS, D, D_FF_EXPERT), D ** -0.5),
        "e_w2": nrm(ks[21], (DEPTH, N_EXPERTS, D_FF_EXPERT, D), D_FF_EXPERT ** -0.5),
        "g_final": 1.0 + nrm(ks[22], (D,), 0.02),
    }


def reference(x, c, ctx, c_ctx, w_mod, b_mod, g_norm_mix, g_norm_ffn,
              a_w_in, a_b_in, a_ln_g, a_ln_b, a_w_s, a_b_s, a_w_out,
              b_w_qkv, b_sink, b_w_o, r_w, e_w1, e_w3, e_w2, g_final):
    L = x.shape[1]
    ROWS = L // GRID_W
    row = jnp.broadcast_to(jnp.arange(ROWS)[:, None], (ROWS, GRID_W)).reshape(-1).astype(jnp.float32)
    col = jnp.broadcast_to(jnp.arange(GRID_W)[None, :], (ROWS, GRID_W)).reshape(-1).astype(jnp.float32)
    inv_freq = ROPE_THETA ** (-jnp.arange(ROPE_PAIRS_PER_AXIS, dtype=jnp.float32) / ROPE_PAIRS_PER_AXIS)
    ang = jnp.concatenate([row[:, None] * inv_freq, col[:, None] * inv_freq], axis=-1)
    cos = jnp.cos(ang).astype(x.dtype)
    sin = jnp.sin(ang).astype(x.dtype)

    sc_lat = jax.nn.silu(c)
    sc_ctx = jax.nn.silu(c_ctx)
    xc = ctx
    for i in range(DEPTH):
        last = i == DEPTH - 1
        is_attn = i % 2 == 1
        j = i // 2
        mod_l = sc_lat @ w_mod[i] + b_mod[i]
        mod_c = sc_ctx @ w_mod[i] + b_mod[i]
        sh1, s1, g1, sh2, s2, g2 = jnp.split(mod_l[:, None, :], 6, axis=-1)
        csh1, cs1, cg1, csh2, cs2, cg2 = jnp.split(mod_c, 6, axis=-1)

        h_l = rmsnorm(x, g_norm_mix[i]) * (1.0 + s1) + sh1
        need_ctx_in = (not last) or is_attn
        h_c = rmsnorm(xc, g_norm_mix[i]) * (1.0 + cs1) + csh1 if need_ctx_in else None
        if is_attn:
            o_l, o_c = windowed_gqa(h_l, h_c, b_w_qkv[j], b_sink[j], b_w_o[j], cos, sin, not last)
        else:
            o_l = chunk_gmlp(h_l, a_w_in[j], a_b_in[j], a_ln_g[j], a_ln_b[j], a_w_s[j], a_b_s[j], a_w_out[j])
            o_c = (chunk_gmlp(h_c, a_w_in[j], a_b_in[j], a_ln_g[j], a_ln_b[j], a_w_s[j], a_b_s[j], a_w_out[j])
                   if not last else None)
        x = x + g1 * o_l

        f_l = rmsnorm(x, g_norm_ffn[i]) * (1.0 + s2) + sh2
        x = x + g2 * expert_choice_ffn(f_l, r_w[i], e_w1[i], e_w3[i], e_w2[i])
        if not last:
            xc = xc + cg1 * o_c
            f_c = rmsnorm(xc, g_norm_ffn[i]) * (1.0 + cs2) + csh2
            xc = xc + cg2 * expert_choice_ffn(f_c, r_w[i], e_w1[i], e_w3[i], e_w2[i])
    return rmsnorm(x, g_final)
```

```python
import functools

import jax
import jax.numpy as jnp
from jax import lax
from jax.experimental import pallas as pl
from jax.experimental.pallas import tpu as pltpu

GRID_W = 64
CHUNK = 128
GA = 16
HEAD_DIM = 64
N_HEADS = 16
N_KV_HEADS = 4
Q_PER_KV = N_HEADS // N_KV_HEADS
WINDOW = 128
BLOCK = 128
ROPE_THETA = 10000.0
N_EXPERTS = 16
CAPACITY_FACTOR = 2
EPS = 1e-6

LANES = 128
SUBLANES = 8
TM = 256
MOD_ROWS = 24
VMEM_LIMIT = 56 * 1024 * 1024

BF16 = jnp.bfloat16
F32 = jnp.float32


def _cparams(sem, vmem=VMEM_LIMIT):
    return pltpu.CompilerParams(dimension_semantics=sem, vmem_limit_bytes=vmem)


def _load_rows(ref, rows):
    return jnp.concatenate(
        [ref[pl.ds(c, rows, stride=SUBLANES), :] for c in range(SUBLANES)], axis=1)


def _store_rows(ref, val, rows):
    for c in range(SUBLANES):
        ref[pl.ds(c, rows, stride=SUBLANES), :] = val[:, c * LANES:(c + 1) * LANES]


def _rms_mod(x, g, scale, shift):
    y = x * lax.rsqrt(jnp.mean(x * x, axis=-1, keepdims=True) + EPS)
    return (y * g) * (1.0 + scale) + shift


def _split(x):
    hi = x.astype(BF16)
    lo = (x - hi.astype(F32)).astype(BF16)
    return hi, lo


def _dot(a, b):
    return jnp.dot(a, b, preferred_element_type=F32)


def _dot_nt(a, b):
    return lax.dot_general(a, b, (((1,), (1,)), ((), ())), preferred_element_type=F32)


def _mod_kernel(c_ref, w_ref, b_ref, o_ref):
    c = c_ref[...]
    s = c * jax.nn.sigmoid(c)
    s_hi, s_lo = _split(s)
    w_hi, w_lo = _split(w_ref[...])
    o_ref[...] = _dot(s_hi, w_hi) + _dot(s_lo, w_hi) + _dot(s_hi, w_lo) + b_ref[...]


def _modulation(cvec, w_mod, b_mod):
    depth, d, d6 = w_mod.shape
    tn = 1024
    return pl.pallas_call(
        _mod_kernel,
        grid=(depth, d6 // tn),
        in_specs=[
            pl.BlockSpec((MOD_ROWS, d), lambda i, j: (0, 0)),
            pl.BlockSpec((None, d, tn), lambda i, j: (i, 0, j)),
            pl.BlockSpec((None, 1, tn), lambda i, j: (i, 0, j)),
        ],
        out_specs=pl.BlockSpec((None, MOD_ROWS, tn), lambda i, j: (i, 0, j)),
        out_shape=jax.ShapeDtypeStruct((depth, MOD_ROWS, d6), F32),
        compiler_params=_cparams(("arbitrary", "arbitrary")),
        name="modulation",
    )(cvec, w_mod, b_mod.reshape(depth, 1, d6))


def _gmlp_kernel(x_ref, mod_ref, gn_ref, win_ref, bin_ref, lng_ref, lnb_ref, ws_ref, bst_ref,
                 wout_ref, o_ref, *, da):
    x = _load_rows(x_ref, TM)
    m = mod_ref[...]
    h = _rms_mod(x, gn_ref[...], m[1:2], m[0:1]).astype(BF16)
    u = jax.nn.gelu(_dot(h, win_ref[:, :da]) + bin_ref[:, :da])
    v = jax.nn.gelu(_dot(h, win_ref[:, da:]) + bin_ref[:, da:])
    mu = jnp.mean(v, axis=-1, keepdims=True)
    var = jnp.mean(jnp.square(v - mu), axis=-1, keepdims=True)
    v = ((v - mu) * lax.rsqrt(var + EPS) * lng_ref[...] + lnb_ref[...]).astype(BF16)
    cw = da // GA
    rows = []
    for r in range(TM // CHUNK):
        cols = []
        for g in range(GA):
            blk = v[r * CHUNK:(r + 1) * CHUNK, g * cw:(g + 1) * cw]
            cols.append(_dot(ws_ref[g], blk) + bst_ref[:, g:g + 1])
        rows.append(jnp.concatenate(cols, axis=1))
    sv = jnp.concatenate(rows, axis=0)
    o = _dot((u * sv).astype(BF16), wout_ref[...])
    _store_rows(o_ref, x + m[2:3] * o, TM)


def _gmlp(xa, mod, gn, w_in, b_in, ln_g, ln_b, w_s, b_s, w_out, *, tps, lat_tiles, n_samples):
    d = gn.shape[-1]
    da = ln_g.shape[-1]
    n_tiles = xa.shape[0] // (TM * SUBLANES)

    def mod_idx(t):
        return (jnp.where(t % tps < lat_tiles, t // tps, n_samples), 0, 0)

    const2 = lambda t: (0, 0)
    return pl.pallas_call(
        functools.partial(_gmlp_kernel, da=da),
        grid=(n_tiles,),
        in_specs=[
            pl.BlockSpec((TM * SUBLANES, LANES), lambda t: (t, 0)),
            pl.BlockSpec((None, SUBLANES, d), mod_idx),
            pl.BlockSpec((1, d), const2),
            pl.BlockSpec((d, 2 * da), const2),
            pl.BlockSpec((1, 2 * da), const2),
            pl.BlockSpec((1, da), const2),
            pl.BlockSpec((1, da), const2),
            pl.BlockSpec((GA, CHUNK, CHUNK), lambda t: (0, 0, 0)),
            pl.BlockSpec((CHUNK, GA), const2),
            pl.BlockSpec((da, d), const2),
        ],
        out_specs=pl.BlockSpec((TM * SUBLANES, LANES), lambda t: (t, 0)),
        out_shape=jax.ShapeDtypeStruct(xa.shape, F32),
        compiler_params=_cparams(("arbitrary",)),
        name="gmlp_mixer",
    )(xa, mod, gn.reshape(1, d), w_in.astype(BF16), b_in.reshape(1, -1), ln_g.reshape(1, -1),
      ln_b.reshape(1, -1), w_s.astype(BF16), b_s.T, w_out.astype(BF16))


def _qkv_kernel(x_ref, mod_ref, gn_ref, w_ref, cos_ref, sin_ref, q_ref, k_ref, v_ref, *, nq, nk):
    x = _load_rows(x_ref, TM)
    m = mod_ref[...]
    h = _rms_mod(x, gn_ref[...], m[1:2], m[0:1]).astype(BF16)
    qkv = _dot(h, w_ref[...])
    cs = cos_ref[...]
    sn = sin_ref[...]
    lane = lax.broadcasted_iota(jnp.int32, (TM, LANES), 1)
    first_half = (lane % HEAD_DIM) < (HEAD_DIM // 2)

    def rope(t):
        sw = jnp.where(first_half, pltpu.roll(t, LANES - HEAD_DIM // 2, 1),
                       pltpu.roll(t, HEAD_DIM // 2, 1))
        return t * cs + sw * sn

    scale = HEAD_DIM ** -0.5
    for j in range(nq // LANES):
        q_ref[:, j * LANES:(j + 1) * LANES] = (
            rope(qkv[:, j * LANES:(j + 1) * LANES]) * scale).astype(BF16)
    for j in range(nk // LANES):
        k_ref[:, j * LANES:(j + 1) * LANES] = rope(
            qkv[:, nq + j * LANES:nq + (j + 1) * LANES]).astype(BF16)
    v_ref[...] = qkv[:, nq + nk:].astype(BF16)


def _qkv(xa, mod, gn, w_qkv, cos_t, sin_t, *, tps, lat_tiles, n_samples):
    d = gn.shape[-1]
    nq = N_HEADS * HEAD_DIM
    nk = N_KV_HEADS * HEAD_DIM
    n_tok = xa.shape[0] // SUBLANES
    n_tiles = n_tok // TM

    def mod_idx(t):
        return (jnp.where(t % tps < lat_tiles, t // tps, n_samples), 0, 0)

    const2 = lambda t: (0, 0)
    return pl.pallas_call(
        functools.partial(_qkv_kernel, nq=nq, nk=nk),
        grid=(n_tiles,),
        in_specs=[
            pl.BlockSpec((TM * SUBLANES, LANES), lambda t: (t, 0)),
            pl.BlockSpec((None, SUBLANES, d), mod_idx),
            pl.BlockSpec((1, d), const2),
            pl.BlockSpec((d, nq + 2 * nk), const2),
            pl.BlockSpec((TM, LANES), lambda t: (t % tps, 0)),
            pl.BlockSpec((TM, LANES), lambda t: (t % tps, 0)),
        ],
        out_specs=[
            pl.BlockSpec((TM, nq), lambda t: (t, 0)),
            pl.BlockSpec((TM, nk), lambda t: (t, 0)),
            pl.BlockSpec((TM, nk), lambda t: (t, 0)),
        ],
        out_shape=[
            jax.ShapeDtypeStruct((n_tok, nq), BF16),
            jax.ShapeDtypeStruct((n_tok, nk), BF16),
            jax.ShapeDtypeStruct((n_tok, nk), BF16),
        ],
        compiler_params=_cparams(("arbitrary",)),
        name="qkv_rope",
    )(xa, mod, gn.reshape(1, d), w_qkv.astype(BF16), cos_t, sin_t)


def _attn_kernel(q_ref, k0_ref, k1_ref, k2_ref, v0_ref, v1_ref, v2_ref, kc_ref, vc_ref, sink_ref,
                 wo_ref, x_ref, mod_ref, o_ref, *, nb, seq):
    i = pl.program_id(1)
    q = q_ref[...]
    kw = jnp.concatenate([k0_ref[...], k1_ref[...], k2_ref[...]], axis=0)
    vw = jnp.concatenate([v0_ref[...], v1_ref[...], v2_ref[...]], axis=0)
    kc = kc_ref[...]
    vc = vc_ref[...]
    iq = lax.broadcasted_iota(jnp.int32, (BLOCK, 3 * BLOCK), 0)
    jk = lax.broadcasted_iota(jnp.int32, (BLOCK, 3 * BLOCK), 1)
    pos = (i - 1) * BLOCK + jk
    mask = (jnp.abs(jk - BLOCK - iq) <= WINDOW) & (pos >= 0) & (pos < seq) & (i < nb)
    mask = jnp.concatenate([mask] * Q_PER_KV, axis=0)
    sink = sink_ref[...]
    outs = [None] * N_HEADS
    for kh in range(N_KV_HEADS):
        heads = [kh * Q_PER_KV + g for g in range(Q_PER_KV)]
        qh = jnp.concatenate([q[:, h * HEAD_DIM:(h + 1) * HEAD_DIM] for h in heads], axis=0)
        sk = jnp.concatenate(
            [jnp.broadcast_to(sink[h:h + 1, 0:1], (BLOCK, 1)) for h in heads], axis=0)
        sl = slice(kh * HEAD_DIM, (kh + 1) * HEAD_DIM)
        s_w = jnp.where(mask, _dot_nt(qh, kw[:, sl]), -jnp.inf)
        s_c = _dot_nt(qh, kc[:, sl])
        mx = jnp.maximum(jnp.maximum(jnp.max(s_w, axis=-1, keepdims=True),
                                     jnp.max(s_c, axis=-1, keepdims=True)), sk)
        p_w = jnp.exp(s_w - mx)
        p_c = jnp.exp(s_c - mx)
        den = (jnp.sum(p_w, axis=-1, keepdims=True) + jnp.sum(p_c, axis=-1, keepdims=True)
               + jnp.exp(sk - mx))
        o = (_dot(p_w.astype(BF16), vw[:, sl]) + _dot(p_c.astype(BF16), vc[:, sl])) / den
        for g, h in enumerate(heads):
            outs[h] = o[g * BLOCK:(g + 1) * BLOCK]
    o_all = jnp.concatenate(outs, axis=1).astype(BF16)
    x = _load_rows(x_ref, BLOCK)
    _store_rows(o_ref, x + mod_ref[2:3] * _dot(o_all, wo_ref[...]), BLOCK)


def _attention(xa, q, k, v, mod, sink, w_o, *, n_samples, seq, n_ctx, ctx_out):
    d = w_o.shape[-1]
    nq = N_HEADS * HEAD_DIM
    nk = N_KV_HEADS * HEAD_DIM
    nt = seq + n_ctx
    bps = nt // BLOCK
    nb = seq // BLOCK
    nqb = bps if ctx_out else nb

    def kv_idx(off):
        return lambda b, i: (b * bps + jnp.clip(i + off, 0, nb - 1), 0)

    ctx_idx = lambda b, i: ((b * nt + seq) // n_ctx, 0)
    kv_spec = [pl.BlockSpec((BLOCK, nk), kv_idx(o)) for o in (-1, 0, 1)]
    row_idx = lambda b, i: (b * bps + i, 0)
    return pl.pallas_call(
        functools.partial(_attn_kernel, nb=nb, seq=seq),
        grid=(n_samples, nqb),
        in_specs=[pl.BlockSpec((BLOCK, nq), row_idx)] + kv_spec + kv_spec + [
            pl.BlockSpec((n_ctx, nk), ctx_idx),
            pl.BlockSpec((n_ctx, nk), ctx_idx),
            pl.BlockSpec((N_HEADS, LANES), lambda b, i: (0, 0)),
            pl.BlockSpec((nq, d), lambda b, i: (0, 0)),
            pl.BlockSpec((BLOCK * SUBLANES, LANES), row_idx),
            pl.BlockSpec((None, SUBLANES, d),
                         lambda b, i: (jnp.where(i < nb, b, n_samples), 0, 0)),
        ],
        out_specs=pl.BlockSpec((BLOCK * SUBLANES, LANES), row_idx),
        out_shape=jax.ShapeDtypeStruct(xa.shape, F32),
        input_output_aliases={11: 0},
        compiler_params=_cparams(("arbitrary", "arbitrary")),
        name="window_attention",
    )(q, k, k, k, v, v, v, k, v, jnp.broadcast_to(sink.astype(F32)[:, None], (N_HEADS, LANES)),
      w_o.astype(BF16), xa, mod)


def _route_kernel(x_ref, mod_ref, gn_ref, wr_ref, key_ref, aff_ref, aff_s, *, n_tiles, cap):
    t = pl.program_id(1)
    x = _load_rows(x_ref, TM)
    m = mod_ref[...]
    f = _rms_mod(x, gn_ref[...], m[4:5], m[3:4])
    f_hi, f_lo = _split(f)
    w_hi, w_lo = _split(wr_ref[...])
    logits = _dot_nt(w_hi, f_hi) + _dot_nt(w_lo, f_hi) + _dot_nt(w_hi, f_lo)
    ex = jnp.exp(logits - jnp.max(logits, axis=0, keepdims=True))
    aff_s[t] = ex / jnp.sum(ex, axis=0, keepdims=True)

    @pl.when(t == n_tiles - 1)
    def _():
        n = n_tiles * TM
        aff = jnp.concatenate([aff_s[j] for j in range(n_tiles)], axis=1)
        bits = pltpu.bitcast(aff, jnp.int32)
        tok = lax.broadcasted_iota(jnp.int32, aff.shape, 1)

        def count(pred):
            return jnp.sum(pred.astype(jnp.int32), axis=1, keepdims=True)

        def value_step(s, thr):
            cand = thr | jnp.left_shift(jnp.int32(1), 30 - s)
            return jnp.where(count(bits >= cand) >= cap, cand, thr)

        thr = lax.fori_loop(0, 31, value_step, jnp.zeros((aff.shape[0], 1), jnp.int32))
        above = bits > thr
        tied = bits == thr
        need = cap - count(above)

        def index_step(s, lim):
            cand = lim | jnp.left_shift(jnp.int32(1), n.bit_length() - 1 - s)
            return jnp.where(count(tied & (tok < cand)) < need, cand, lim)

        lim = lax.fori_loop(0, n.bit_length(), index_step,
                            jnp.zeros((aff.shape[0], 1), jnp.int32))
        chosen = above | (tied & (tok <= lim))
        key_ref[...] = jnp.where(chosen, tok, tok + n)
        aff_ref[...] = aff


def _route(xa, mod, gn, w_r_t, *, n_samples, tps, tile_off, n_tiles, mod_row, cap):
    d = gn.shape[-1]
    n = n_tiles * TM
    e = w_r_t.shape[0]
    mod_idx = (lambda b, t: (b, 0, 0)) if mod_row is None else (lambda b, t: (mod_row, 0, 0))
    return pl.pallas_call(
        functools.partial(_route_kernel, n_tiles=n_tiles, cap=cap),
        grid=(n_samples, n_tiles),
        in_specs=[
            pl.BlockSpec((TM * SUBLANES, LANES), lambda b, t: (b * tps + tile_off + t, 0)),
            pl.BlockSpec((None, SUBLANES, d), mod_idx),
            pl.BlockSpec((1, d), lambda b, t: (0, 0)),
            pl.BlockSpec((e, d), lambda b, t: (0, 0)),
        ],
        out_specs=[
            pl.BlockSpec((None, e, n), lambda b, t: (b, 0, 0)),
            pl.BlockSpec((None, e, n), lambda b, t: (b, 0, 0)),
        ],
        out_shape=[
            jax.ShapeDtypeStruct((n_samples, e, n), jnp.int32),
            jax.ShapeDtypeStruct((n_samples, e, n), F32),
        ],
        scratch_shapes=[pltpu.VMEM((n_tiles, e, TM), F32)],
        compiler_params=_cparams(("arbitrary", "arbitrary")),
        name="route_topk",
    )(xa, mod, gn.reshape(1, d), w_r_t)


def _gather_kernel(idx_hbm, x_ref, modb_ref, modc_ref, gn_ref, o_ref, idx_s, stg, sem, *, r, rl):
    b = pl.program_id(0)
    e = pl.program_id(1)

    @pl.when(e == 0)
    def _():
        cp = pltpu.make_async_copy(idx_hbm.at[b], idx_s, sem)
        cp.start()
        cp.wait()

    base = e * r

    def body(j, carry):
        n = idx_s[base + j]
        src = pl.multiple_of(n * SUBLANES, SUBLANES)
        dst = pl.multiple_of(j * SUBLANES, SUBLANES)
        stg[pl.ds(dst, SUBLANES), :] = x_ref[pl.ds(src, SUBLANES), :]
        return carry

    lax.fori_loop(0, r, body, 0, unroll=8)
    xg = _load_rows(stg, r)
    gn = gn_ref[...]
    mb = modb_ref[...]
    f = _rms_mod(xg[:rl], gn, mb[4:5], mb[3:4]).astype(BF16)
    if r > rl:
        mc = modc_ref[...]
        fc = _rms_mod(xg[rl:], gn, mc[4:5], mc[3:4]).astype(BF16)
        f = jnp.concatenate([f, fc], axis=0)
    o_ref[...] = f


def _gather(idx, xa, mod, gn, *, n_samples, nt, r, rl):
    d = gn.shape[-1]
    e = N_EXPERTS
    return pl.pallas_call(
        functools.partial(_gather_kernel, r=r, rl=rl),
        grid=(n_samples, e),
        in_specs=[
            pl.BlockSpec(memory_space=pl.ANY),
            pl.BlockSpec((nt * SUBLANES, LANES), lambda b, j: (b, 0)),
            pl.BlockSpec((None, SUBLANES, d), lambda b, j: (b, 0, 0)),
            pl.BlockSpec((None, SUBLANES, d), lambda b, j: (n_samples, 0, 0)),
            pl.BlockSpec((1, d), lambda b, j: (0, 0)),
        ],
        out_specs=pl.BlockSpec((None, None, r, d), lambda b, j: (b, j, 0, 0)),
        out_shape=jax.ShapeDtypeStruct((n_samples, e, r, d), BF16),
        scratch_shapes=[
            pltpu.SMEM((idx.shape[1],), jnp.int32),
            pltpu.VMEM((r * SUBLANES, LANES), F32),
            pltpu.SemaphoreType.DMA,
        ],
        compiler_params=_cparams(("arbitrary", "arbitrary")),
        name="moe_gather",
    )(idx, xa, mod, mod, gn.reshape(1, d))


def _expert_kernel(xg_ref, gate_ref, w1_ref, w3_ref, w2_ref, y_ref, *, r, fc):
    xg = xg_ref[...]
    f = w1_ref.shape[-1]
    acc = None
    for c in range(f // fc):
        sl = slice(c * fc, (c + 1) * fc)
        a = _dot(xg, w1_ref[:, sl])
        g = _dot(xg, w3_ref[:, sl])
        h = (a * jax.nn.sigmoid(a) * g).astype(BF16)
        part = _dot(h, w2_ref[sl, :])
        acc = part if acc is None else acc + part
    _store_rows(y_ref, acc * gate_ref[...], r)


def _experts(xg, gates, w1, w3, w2, layer):
    n_samples, e, r, d = xg.shape
    f = w1.shape[-1]
    return pl.pallas_call(
        functools.partial(_expert_kernel, r=r, fc=512),
        grid=(e, n_samples),
        in_specs=[
            pl.BlockSpec((None, None, r, d), lambda j, b: (b, j, 0, 0)),
            pl.BlockSpec((None, None, r, 1), lambda j, b: (b, j, 0, 0)),
            pl.BlockSpec((None, None, d, f), lambda j, b: (layer, j, 0, 0)),
            pl.BlockSpec((None, None, d, f), lambda j, b: (layer, j, 0, 0)),
            pl.BlockSpec((None, None, f, d), lambda j, b: (layer, j, 0, 0)),
        ],
        out_specs=pl.BlockSpec((None, None, r * SUBLANES, LANES), lambda j, b: (b, j, 0, 0)),
        out_shape=jax.ShapeDtypeStruct((n_samples, e, r * SUBLANES, LANES), F32),
        compiler_params=_cparams(("arbitrary", "arbitrary")),
        name="moe_experts",
    )(xg, gates, w1, w3, w2)


def _combine_kernel(idx_hbm, x_hbm, y_ref, g2b_ref, g2c_ref, o_hbm, idx_s, acc, sem, *,
                    r, rl, rows):
    b = pl.program_id(0)
    e = pl.program_id(1)
    n_exp = pl.num_programs(1)

    @pl.when(e == 0)
    def _():
        cp_i = pltpu.make_async_copy(idx_hbm.at[b], idx_s, sem.at[0])
        cp_x = pltpu.make_async_copy(x_hbm.at[pl.ds(b * rows, rows), :], acc, sem.at[1])
        cp_i.start()
        cp_x.start()
        cp_i.wait()
        cp_x.wait()

    base = e * r
    group = 8

    def add_rows(j0, gate):
        ns = [pl.multiple_of(idx_s[base + j0 + u] * SUBLANES, SUBLANES) for u in range(group)]
        vals = [acc[pl.ds(ns[u], SUBLANES), :]
                + y_ref[pl.ds(pl.multiple_of((j0 + u) * SUBLANES, SUBLANES), SUBLANES), :] * gate
                for u in range(group)]
        for u in range(group):
            acc[pl.ds(ns[u], SUBLANES), :] = vals[u]

    g2b = g2b_ref[...]

    def lat_body(jj, carry):
        add_rows(jj * group, g2b)
        return carry

    lax.fori_loop(0, rl // group, lat_body, 0)
    if r > rl:
        g2c = g2c_ref[...]

        def ctx_body(jj, carry):
            add_rows(rl + jj * group, g2c)
            return carry

        lax.fori_loop(0, (r - rl) // group, ctx_body, 0)

    @pl.when(e == n_exp - 1)
    def _():
        cp_o = pltpu.make_async_copy(acc, o_hbm.at[pl.ds(b * rows, rows), :], sem.at[2])
        cp_o.start()
        cp_o.wait()


def _combine(idx, xa, y, g2, *, n_samples, nt, r, rl):
    e = N_EXPERTS
    rows = nt * SUBLANES
    return pl.pallas_call(
        functools.partial(_combine_kernel, r=r, rl=rl, rows=rows),
        grid=(n_samples, e),
        in_specs=[
            pl.BlockSpec(memory_space=pl.ANY),
            pl.BlockSpec(memory_space=pl.ANY),
            pl.BlockSpec((None, None, r * SUBLANES, LANES), lambda b, j: (b, j, 0, 0)),
            pl.BlockSpec((None, SUBLANES, LANES), lambda b, j: (b, 0, 0)),
            pl.BlockSpec((None, SUBLANES, LANES), lambda b, j: (n_samples, 0, 0)),
        ],
        out_specs=pl.BlockSpec(memory_space=pl.ANY),
        out_shape=jax.ShapeDtypeStruct(xa.shape, F32),
        scratch_shapes=[
            pltpu.SMEM((idx.shape[1],), jnp.int32),
            pltpu.VMEM((rows, LANES), F32),
            pltpu.SemaphoreType.DMA((3,)),
        ],
        compiler_params=_cparams(("arbitrary", "arbitrary")),
        name="moe_combine",
    )(idx, xa, y, g2, g2)


def _final_kernel(x_ref, g_ref, o_ref):
    x = _load_rows(x_ref, TM)
    o_ref[...] = x * lax.rsqrt(jnp.mean(x * x, axis=-1, keepdims=True) + EPS) * g_ref[...]


def _final_norm(xa, g, *, n_samples, seq, tps):
    d = g.shape[-1]
    lat_tiles = seq // TM
    return pl.pallas_call(
        _final_kernel,
        grid=(n_samples * lat_tiles,),
        in_specs=[
            pl.BlockSpec((TM * SUBLANES, LANES),
                         lambda t: ((t // lat_tiles) * tps + t % lat_tiles, 0)),
            pl.BlockSpec((1, d), lambda t: (0, 0)),
        ],
        out_specs=pl.BlockSpec((TM, d), lambda t: (t, 0)),
        out_shape=jax.ShapeDtypeStruct((n_samples * seq, d), F32),
        compiler_params=_cparams(("arbitrary",)),
        name="final_norm",
    )(xa, g.reshape(1, d))


def _select(key, aff, cap):
    skey, saff = lax.sort((key, aff), dimension=2, num_keys=1)
    return skey[:, :, :cap], saff[:, :, :cap]


def kernel(x, c, ctx, c_ctx, w_mod, b_mod, g_norm_mix, g_norm_ffn, a_w_in, a_b_in, a_ln_g, a_ln_b,
           a_w_s, a_b_s, a_w_out, b_w_qkv, b_sink, b_w_o, r_w, e_w1, e_w3, e_w2, g_final):
    n_samples, seq, d = x.shape
    n_ctx = ctx.shape[1]
    depth = w_mod.shape[0]
    n_exp = r_w.shape[-1]
    assert d == SUBLANES * LANES and n_exp == N_EXPERTS
    assert seq % TM == 0 and n_ctx == TM and n_samples < MOD_ROWS
    nt = seq + n_ctx
    tps = nt // TM
    lat_tiles = seq // TM
    cap_l = (CAPACITY_FACTOR * seq) // n_exp
    cap_c = (CAPACITY_FACTOR * n_ctx) // n_exp

    xa = jnp.concatenate([x, ctx], axis=1).reshape(n_samples * nt * SUBLANES, LANES)

    rows = seq // GRID_W
    row = jnp.broadcast_to(jnp.arange(rows)[:, None], (rows, GRID_W)).reshape(-1).astype(F32)
    col = jnp.broadcast_to(jnp.arange(GRID_W)[None, :], (rows, GRID_W)).reshape(-1).astype(F32)
    pairs = HEAD_DIM // 4
    inv_freq = ROPE_THETA ** (-jnp.arange(pairs, dtype=F32) / pairs)
    ang = jnp.concatenate([row[:, None] * inv_freq, col[:, None] * inv_freq], axis=-1)
    cos = jnp.cos(ang)
    sin = jnp.sin(ang)
    cos_t = jnp.concatenate([jnp.tile(cos, (1, 4)), jnp.ones((n_ctx, LANES), F32)], axis=0)
    sin_t = jnp.concatenate([jnp.tile(jnp.concatenate([-sin, sin], axis=1), (1, 2)),
                             jnp.zeros((n_ctx, LANES), F32)], axis=0)

    cvec = jnp.zeros((MOD_ROWS, d), F32).at[:n_samples].set(c).at[n_samples].set(c_ctx)
    mod_all = _modulation(cvec, w_mod, b_mod).reshape(depth, MOD_ROWS, 6, d)
    mod_all = jnp.pad(mod_all, ((0, 0), (0, 0), (0, SUBLANES - 6), (0, 0)))

    w1 = e_w1.astype(BF16)
    w3 = e_w3.astype(BF16)
    w2 = e_w2.astype(BF16)

    for i in range(depth):
        last = i == depth - 1
        is_attn = i % 2 == 1
        j = i // 2
        mod = mod_all[i]
        kw = dict(tps=tps, lat_tiles=lat_tiles, n_samples=n_samples)
        if is_attn:
            q, k, v = _qkv(xa, mod, g_norm_mix[i], b_w_qkv[j], cos_t, sin_t, **kw)
            xa = _attention(xa, q, k, v, mod, b_sink[j], b_w_o[j], n_samples=n_samples, seq=seq,
                            n_ctx=n_ctx, ctx_out=not last)
        else:
            xa = _gmlp(xa, mod, g_norm_mix[i], a_w_in[j], a_b_in[j], a_ln_g[j], a_ln_b[j],
                       a_w_s[j], a_b_s[j], a_w_out[j], **kw)

        w_r_t = r_w[i].T
        key, aff = _route(xa, mod, g_norm_ffn[i], w_r_t, n_samples=n_samples, tps=tps, tile_off=0,
                          n_tiles=lat_tiles, mod_row=None, cap=cap_l)
        idx, gates = _select(key, aff, cap_l)
        if not last:
            key_c, aff_c = _route(xa, mod, g_norm_ffn[i], w_r_t, n_samples=n_samples, tps=tps,
                                  tile_off=lat_tiles, n_tiles=n_ctx // TM, mod_row=n_samples,
                                  cap=cap_c)
            idx_c, gates_c = _select(key_c, aff_c, cap_c)
            idx = jnp.concatenate([idx, idx_c + seq], axis=2)
            gates = jnp.concatenate([gates, gates_c], axis=2)
        r = idx.shape[2]
        flat = idx.reshape(n_samples, n_exp * r)
        flat = jnp.pad(flat, ((0, 0), (0, -flat.shape[1] % 1024)))
        xg = _gather(flat, xa, mod, g_norm_ffn[i], n_samples=n_samples, nt=nt, r=r, rl=cap_l)
        y = _experts(xg, gates[..., None], w1, w3, w2, i)
        g2 = mod[:, 5].reshape(MOD_ROWS, SUBLANES, LANES)
        xa = _combine(flat, xa, y, g2, n_samples=n_samples, nt=nt, r=r, rl=cap_l)

    out = _final_norm(xa, g_final, n_samples=n_samples, seq=seq, tps=tps)
    return out.reshape(n_samples, seq, d)
```

```python
import functools

import jax
import jax.numpy as jnp
from jax import lax
from jax.experimental import pallas as pl
from jax.experimental.pallas import tpu as pltpu

GRID_W = 64
CHUNK = 128
GA = 16
HEAD_DIM = 64
N_HEADS = 16
N_KV_HEADS = 4
Q_PER_KV = N_HEADS // N_KV_HEADS
WINDOW = 128
BLOCK = 128
ROPE_THETA = 10000.0
N_EXPERTS = 16
CAPACITY_FACTOR = 2
EPS = 1e-6

LANES = 128
SUBLANES = 8
TM = 256
MOD_ROWS = 24
ATTN_LOOKAHEAD = 3
GMLP_COLS = 512
GMLP_LOOKAHEAD = 2
VMEM_LIMIT = 56 * 1024 * 1024

BF16 = jnp.bfloat16
F32 = jnp.float32


def _cparams(sem, vmem=VMEM_LIMIT):
    return pltpu.CompilerParams(dimension_semantics=sem, vmem_limit_bytes=vmem)


def _load_rows(ref, rows):
    return jnp.concatenate(
        [ref[pl.ds(c, rows, stride=SUBLANES), :] for c in range(SUBLANES)], axis=1)


def _store_rows(ref, val, rows):
    for c in range(SUBLANES):
        ref[pl.ds(c, rows, stride=SUBLANES), :] = val[:, c * LANES:(c + 1) * LANES]


def _rms_mod(x, g, scale, shift):
    y = x * lax.rsqrt(jnp.mean(x * x, axis=-1, keepdims=True) + EPS)
    return (y * g) * (1.0 + scale) + shift


def _split(x):
    hi = x.astype(BF16)
    lo = (x - hi.astype(F32)).astype(BF16)
    return hi, lo


def _dot(a, b):
    return jnp.dot(a, b, preferred_element_type=F32)


def _dot_nt(a, b):
    return lax.dot_general(a, b, (((1,), (1,)), ((), ())), preferred_element_type=F32)


def _gelu(x):
    k = 0.7978845608028654
    inner = x * ((x * x) * (k * 0.044715) + k)
    hx = 0.5 * x
    return hx * jnp.tanh(inner) + hx


def _route_epilogue(xn, m, gnf_ref, wr_ref, f_ref, aff_ref, rows):
    f = _rms_mod(xn, gnf_ref[...], m[4:5], m[3:4])
    _store_rows(f_ref, f, rows)
    f_hi, f_lo = _split(f)
    w_hi, w_lo = _split(wr_ref[...])
    logits = _dot_nt(w_hi, f_hi) + _dot_nt(w_lo, f_hi) + _dot_nt(w_hi, f_lo)
    ex = jnp.exp(logits - jnp.max(logits, axis=0, keepdims=True))
    aff = ex / jnp.sum(ex, axis=0, keepdims=True)
    for kk in range(rows // LANES):
        aff_ref[kk] = aff[:, kk * LANES:(kk + 1) * LANES]


def _mod_kernel(c_ref, w_ref, b_ref, o_ref):
    c = c_ref[...]
    s = c * jax.nn.sigmoid(c)
    s_hi, s_lo = _split(s)
    w_hi, w_lo = _split(w_ref[...])
    o_ref[...] = _dot(s_hi, w_hi) + _dot(s_lo, w_hi) + _dot(s_hi, w_lo) + b_ref[...]


def _modulation(cvec, w_mod, b_mod):
    depth, d, d6 = w_mod.shape
    tn = 1024
    return pl.pallas_call(
        _mod_kernel,
        grid=(depth, d6 // tn),
        in_specs=[
            pl.BlockSpec((MOD_ROWS, d), lambda i, j: (0, 0)),
            pl.BlockSpec((None, d, tn), lambda i, j: (i, 0, j)),
            pl.BlockSpec((None, 1, tn), lambda i, j: (i, 0, j)),
        ],
        out_specs=pl.BlockSpec((None, MOD_ROWS, tn), lambda i, j: (i, 0, j)),
        out_shape=jax.ShapeDtypeStruct((depth, MOD_ROWS, d6), F32),
        compiler_params=_cparams(("arbitrary", "arbitrary")),
        name="modulation",
    )(cvec, w_mod, b_mod.reshape(depth, 1, d6))


def _gmlp_kernel(x_ref, mod_ref, gn_ref, win_ref, bin_ref, lng_ref, lnb_ref, ws_ref, bst_ref,
                 wout_ref, gnf_ref, wr_ref, o_ref, f_ref, aff_ref, *, da):
    x = _load_rows(x_ref, TM)
    m = mod_ref[...]
    h = _rms_mod(x, gn_ref[...], m[1:2], m[0:1]).astype(BF16)
    cc = GMLP_COLS
    n_half = da // cc

    def zchunk(j):
        return _dot(h, win_ref[:, j * cc:(j + 1) * cc]) + bin_ref[:, j * cc:(j + 1) * cc]

    order = list(range(n_half, 2 * n_half)) + list(range(n_half))
    pend = [zchunk(j) for j in order[:GMLP_LOOKAHEAD]]
    act = {}
    vn = None
    for t, j in enumerate(order):
        z = pend.pop(0)
        if t + GMLP_LOOKAHEAD < len(order):
            pend.append(zchunk(order[t + GMLP_LOOKAHEAD]))
        act[j] = _gelu(z)
        if t == n_half - 1:
            vs = [act[n_half + q] for q in range(n_half)]
            mu = sum(jnp.sum(a, axis=-1, keepdims=True) for a in vs) / da
            var = sum(jnp.sum(jnp.square(a - mu), axis=-1, keepdims=True) for a in vs) / da
            rs = lax.rsqrt(var + EPS)
            vn = [((a - mu) * rs * lng_ref[:, q * cc:(q + 1) * cc]
                   + lnb_ref[:, q * cc:(q + 1) * cc]).astype(BF16) for q, a in enumerate(vs)]
    cw = da // GA
    gpc = cc // cw

    def spatial(q):
        cols = []
        for gg in range(gpc):
            g = q * gpc + gg
            rows = [_dot(ws_ref[g], vn[q][r * CHUNK:(r + 1) * CHUNK, gg * cw:(gg + 1) * cw])
                    + bst_ref[:, g:g + 1] for r in range(TM // CHUNK)]
            cols.append(jnp.concatenate(rows, axis=0))
        return jnp.concatenate(cols, axis=1)

    o = None
    sv_next = spatial(0)
    for q in range(n_half):
        sv = sv_next
        if q + 1 < n_half:
            sv_next = spatial(q + 1)
        part = _dot((act[q] * sv).astype(BF16), wout_ref[q * cc:(q + 1) * cc, :])
        o = part if o is None else o + part
    xn = x + m[2:3] * o
    _store_rows(o_ref, xn, TM)
    _route_epilogue(xn, m, gnf_ref, wr_ref, f_ref, aff_ref, TM)


def _gmlp(xa, mod, gn, w_in, b_in, ln_g, ln_b, w_s, b_s, w_out, gnf, w_r_t, *, tps, lat_tiles,
          n_samples):
    d = gn.shape[-1]
    da = ln_g.shape[-1]
    n_tiles = xa.shape[0] // (TM * SUBLANES)
    bpt = TM // LANES

    def mod_idx(t):
        return (jnp.where(t % tps < lat_tiles, t // tps, n_samples), 0, 0)

    const2 = lambda t: (0, 0)
    tile = pl.BlockSpec((TM * SUBLANES, LANES), lambda t: (t, 0))
    return pl.pallas_call(
        functools.partial(_gmlp_kernel, da=da),
        grid=(n_tiles,),
        in_specs=[
            tile,
            pl.BlockSpec((None, SUBLANES, d), mod_idx),
            pl.BlockSpec((1, d), const2),
            pl.BlockSpec((d, 2 * da), const2),
            pl.BlockSpec((1, 2 * da), const2),
            pl.BlockSpec((1, da), const2),
            pl.BlockSpec((1, da), const2),
            pl.BlockSpec((GA, CHUNK, CHUNK), lambda t: (0, 0, 0)),
            pl.BlockSpec((CHUNK, GA), const2),
            pl.BlockSpec((da, d), const2),
            pl.BlockSpec((1, d), const2),
            pl.BlockSpec((N_EXPERTS, d), const2),
        ],
        out_specs=[
            tile,
            tile,
            pl.BlockSpec((None, bpt, N_EXPERTS, LANES), lambda t: (t // tps, t % tps, 0, 0)),
        ],
        out_shape=[
            jax.ShapeDtypeStruct(xa.shape, F32),
            jax.ShapeDtypeStruct(xa.shape, F32),
            jax.ShapeDtypeStruct((n_samples, tps * bpt, N_EXPERTS, LANES), F32),
        ],
        compiler_params=_cparams(("arbitrary",)),
        name="gmlp_mixer",
    )(xa, mod, gn.reshape(1, d), w_in.astype(BF16), b_in.reshape(1, -1), ln_g.reshape(1, -1),
      ln_b.reshape(1, -1), w_s.astype(BF16), b_s.T, w_out.astype(BF16), gnf.reshape(1, d), w_r_t)


def _qkv_kernel(x_ref, mod_ref, gn_ref, w_ref, cos_ref, sin_ref, q_ref, k_ref, v_ref, *, nq, nk):
    x = _load_rows(x_ref, TM)
    m = mod_ref[...]
    h = _rms_mod(x, gn_ref[...], m[1:2], m[0:1]).astype(BF16)
    qkv = _dot(h, w_ref[...])
    cs = cos_ref[...]
    sn = sin_ref[...]
    lane = lax.broadcasted_iota(jnp.int32, (TM, LANES), 1)
    first_half = (lane % HEAD_DIM) < (HEAD_DIM // 2)

    def rope(t):
        sw = jnp.where(first_half, pltpu.roll(t, LANES - HEAD_DIM // 2, 1),
                       pltpu.roll(t, HEAD_DIM // 2, 1))
        return t * cs + sw * sn

    scale = HEAD_DIM ** -0.5
    for j in range(nq // LANES):
        qj = rope(qkv[:, j * LANES:(j + 1) * LANES]) * scale
        q_ref[j * LANES:(j + 1) * LANES, :] = qj.T.astype(BF16)
    for j in range(nk // LANES):
        kj = rope(qkv[:, nq + j * LANES:nq + (j + 1) * LANES]).astype(BF16)
        for u in range(LANES // HEAD_DIM):
            k_ref[j * (LANES // HEAD_DIM) + u] = kj[:, u * HEAD_DIM:(u + 1) * HEAD_DIM]
        vj = qkv[:, nq + nk + j * LANES:nq + nk + (j + 1) * LANES]
        v_ref[j * LANES:(j + 1) * LANES, :] = vj.T.astype(BF16)


def _qkv(xa, mod, gn, w_qkv, cos_t, sin_t, *, tps, lat_tiles, n_samples):
    d = gn.shape[-1]
    nq = N_HEADS * HEAD_DIM
    nk = N_KV_HEADS * HEAD_DIM
    n_tok = xa.shape[0] // SUBLANES
    n_tiles = n_tok // TM

    def mod_idx(t):
        return (jnp.where(t % tps < lat_tiles, t // tps, n_samples), 0, 0)

    const2 = lambda t: (0, 0)
    return pl.pallas_call(
        functools.partial(_qkv_kernel, nq=nq, nk=nk),
        grid=(n_tiles,),
        in_specs=[
            pl.BlockSpec((TM * SUBLANES, LANES), lambda t: (t, 0)),
            pl.BlockSpec((None, SUBLANES, d), mod_idx),
            pl.BlockSpec((1, d), const2),
            pl.BlockSpec((d, nq + 2 * nk), const2),
            pl.BlockSpec((TM, LANES), lambda t: (t % tps, 0)),
            pl.BlockSpec((TM, LANES), lambda t: (t % tps, 0)),
        ],
        out_specs=[
            pl.BlockSpec((nq, TM), lambda t: (0, t)),
            pl.BlockSpec((N_KV_HEADS, TM, HEAD_DIM), lambda t: (0, t, 0)),
            pl.BlockSpec((nk, TM), lambda t: (0, t)),
        ],
        out_shape=[
            jax.ShapeDtypeStruct((nq, n_tok), BF16),
            jax.ShapeDtypeStruct((N_KV_HEADS, n_tok, HEAD_DIM), BF16),
            jax.ShapeDtypeStruct((nk, n_tok), BF16),
        ],
        compiler_params=_cparams(("arbitrary",)),
        name="qkv_rope",
    )(xa, mod, gn.reshape(1, d), w_qkv.astype(BF16), cos_t, sin_t)


def _attn_kernel(q_ref, k0_ref, k1_ref, k2_ref, kc_ref, v0_ref, v1_ref, v2_ref, vc_ref, sink_ref,
                 wo_ref, x_ref, mod_ref, gnf_ref, wr_ref, o_ref, f_ref, aff_ref, *, nb, seq):
    i = pl.program_id(1)
    nw = 3 * BLOCK
    jk = lax.broadcasted_iota(jnp.int32, (nw, BLOCK), 0)
    iq = lax.broadcasted_iota(jnp.int32, (nw, BLOCK), 1)
    pos = (i - 1) * BLOCK + jk
    mask = (jnp.abs(jk - BLOCK - iq) <= WINDOW) & (pos >= 0) & (pos < seq) & (i < nb)
    bias = jnp.where(mask, 0.0, -jnp.inf)
    vt = jnp.concatenate([v0_ref[...], v1_ref[...], v2_ref[...], vc_ref[...]], axis=1)
    kks = [jnp.concatenate([k0_ref[kh], k1_ref[kh], k2_ref[kh], kc_ref[kh]], axis=0)
           for kh in range(N_KV_HEADS)]

    def scores(h):
        return _dot(kks[h // Q_PER_KV], q_ref[h * HEAD_DIM:(h + 1) * HEAD_DIM, :])

    def softmax(h, s):
        s_w = s[:nw] + bias
        s_c = s[nw:]
        sk = sink_ref[h:h + 1, :]
        mx = jnp.maximum(jnp.maximum(jnp.max(s_w, axis=0, keepdims=True),
                                     jnp.max(s_c, axis=0, keepdims=True)), sk)
        p_w = jnp.exp(s_w - mx)
        p_c = jnp.exp(s_c - mx)
        den = (jnp.sum(p_w, axis=0, keepdims=True) + jnp.sum(p_c, axis=0, keepdims=True)
               + jnp.exp(sk - mx))
        return jnp.concatenate([p_w, p_c], axis=0).astype(BF16), den

    outs = []
    ss = [scores(h) for h in range(ATTN_LOOKAHEAD)]
    pend = None
    for h in range(N_HEADS):
        s_cur = ss.pop(0)
        if h + ATTN_LOOKAHEAD < N_HEADS:
            ss.append(scores(h + ATTN_LOOKAHEAD))
        p, den = softmax(h, s_cur)
        kh = h // Q_PER_KV
        o_raw = _dot(vt[kh * HEAD_DIM:(kh + 1) * HEAD_DIM], p)
        if pend is not None:
            outs.append(pend[0] / pend[1])
        pend = (o_raw, den)
    outs.append(pend[0] / pend[1])
    o_all = jnp.concatenate(outs, axis=0).T.astype(BF16)
    x = _load_rows(x_ref, BLOCK)
    m = mod_ref[...]
    xn = x + m[2:3] * _dot(o_all, wo_ref[...])
    _store_rows(o_ref, xn, BLOCK)
    _route_epilogue(xn, m, gnf_ref, wr_ref, f_ref, aff_ref, BLOCK)


def _attention(xa, q_t, k4, v_t, mod, sink, w_o, gnf, w_r_t, *, n_samples, seq, n_ctx, ctx_out):
    d = w_o.shape[-1]
    nq = N_HEADS * HEAD_DIM
    nk = N_KV_HEADS * HEAD_DIM
    nt = seq + n_ctx
    bps = nt // BLOCK
    nb = seq // BLOCK
    nqb = bps if ctx_out else nb

    def win(off):
        return lambda b, i: b * bps + jnp.clip(i + off, 0, nb - 1)

    ctx_blk = lambda b, i: (b * nt + seq) // n_ctx
    k_specs = [pl.BlockSpec((N_KV_HEADS, BLOCK, HEAD_DIM), lambda b, i, w=win(o): (0, w(b, i), 0))
               for o in (-1, 0, 1)]
    k_specs.append(pl.BlockSpec((N_KV_HEADS, n_ctx, HEAD_DIM), lambda b, i: (0, ctx_blk(b, i), 0)))
    v_specs = [pl.BlockSpec((nk, BLOCK), lambda b, i, w=win(o): (0, w(b, i))) for o in (-1, 0, 1)]
    v_specs.append(pl.BlockSpec((nk, n_ctx), lambda b, i: (0, ctx_blk(b, i))))
    row_idx = lambda b, i: (b * bps + i, 0)
    return pl.pallas_call(
        functools.partial(_attn_kernel, nb=nb, seq=seq),
        grid=(n_samples, nqb),
        in_specs=[pl.BlockSpec((nq, BLOCK), lambda b, i: (0, b * bps + i))] + k_specs + v_specs + [
            pl.BlockSpec((N_HEADS, LANES), lambda b, i: (0, 0)),
            pl.BlockSpec((nq, d), lambda b, i: (0, 0)),
            pl.BlockSpec((BLOCK * SUBLANES, LANES), row_idx),
            pl.BlockSpec((None, SUBLANES, d),
                         lambda b, i: (jnp.where(i < nb, b, n_samples), 0, 0)),
            pl.BlockSpec((1, d), lambda b, i: (0, 0)),
            pl.BlockSpec((N_EXPERTS, d), lambda b, i: (0, 0)),
        ],
        out_specs=[
            pl.BlockSpec((BLOCK * SUBLANES, LANES), row_idx),
            pl.BlockSpec((BLOCK * SUBLANES, LANES), lambda b, i: (b * nqb + i, 0)),
            pl.BlockSpec((None, 1, N_EXPERTS, LANES), lambda b, i: (b, i, 0, 0)),
        ],
        out_shape=[
            jax.ShapeDtypeStruct(xa.shape, F32),
            jax.ShapeDtypeStruct((n_samples * nqb * BLOCK * SUBLANES, LANES), F32),
            jax.ShapeDtypeStruct((n_samples, nqb, N_EXPERTS, LANES), F32),
        ],
        input_output_aliases={} if ctx_out else {11: 0},
        compiler_params=_cparams(("arbitrary", "arbitrary")),
        name="window_attention",
    )(q_t, k4, k4, k4, k4, v_t, v_t, v_t, v_t,
      jnp.broadcast_to(sink.astype(F32)[:, None], (N_HEADS, LANES)), w_o.astype(BF16), xa, mod,
      gnf.reshape(1, d), w_r_t)


def _select_stream(aff, cin_s, parts_s, cap, tok_off):
    kn, ne, _ = aff.shape
    n = kn * LANES
    capp = -(-cap // LANES) * LANES
    assert kn <= LANES
    bits = pltpu.bitcast(aff, jnp.int32)
    tok = (lax.broadcasted_iota(jnp.int32, aff.shape, 0) * LANES
           + lax.broadcasted_iota(jnp.int32, aff.shape, 2))

    def count(pred):
        return jnp.sum(jnp.sum(pred.astype(jnp.int32), axis=0), axis=1, keepdims=True)

    def value_step(s, thr):
        cand = thr | jnp.left_shift(jnp.int32(1), 30 - s)
        return jnp.where(count(bits >= cand) >= cap, cand, thr)

    thr = lax.fori_loop(0, 31, value_step, jnp.zeros((ne, 1), jnp.int32))
    above = bits > thr
    tied = bits == thr
    need = cap - count(above)

    def index_step(s, lim):
        cand = lim | jnp.left_shift(jnp.int32(1), n.bit_length() - 1 - s)
        return jnp.where(count(tied & (tok < cand)) < need, cand, lim)

    lim = lax.fori_loop(0, n.bit_length(), index_step, jnp.zeros((ne, 1), jnp.int32))
    chosen = above | (tied & (tok <= lim))

    flat = jnp.where(chosen, 1.0, 0.0).reshape(kn * ne, LANES).astype(BF16)
    li = lax.broadcasted_iota(jnp.int32, (LANES, LANES), 0)
    lj = lax.broadcasted_iota(jnp.int32, (LANES, LANES), 1)
    cin = _dot(flat, jnp.where(li <= lj, 1.0, 0.0).astype(BF16))
    cin_s[0:kn * ne, :] = cin
    a1 = aff.astype(BF16).astype(F32)
    a2 = (aff - a1).astype(BF16).astype(F32)
    for i, part in enumerate((a1, a2, aff - a1 - a2)):
        parts_s[i, 0:kn * ne, :] = part.reshape(kn * ne, LANES)

    jrow = lax.broadcasted_iota(jnp.int32, (ne, capp), 1).astype(F32)
    bend = jnp.zeros((ne, 1), F32)
    blk = jnp.zeros((ne, capp), F32)
    before = jnp.zeros((ne, capp), F32)
    for k in range(kn):
        btot = cin[k * ne:(k + 1) * ne, LANES - 1:LANES]
        bend = bend + btot
        full = jnp.where(bend <= jrow, 1.0, 0.0)
        blk = blk + full
        before = before + full * btot
    rank = jrow - before
    sub = lax.broadcasted_iota(jnp.int32, (LANES, capp), 0).astype(F32)

    def block_rows_t(ref, e):
        rows = ref[pl.ds(e, kn, stride=ne), :]
        if kn < LANES:
            rows = jnp.concatenate([rows, jnp.zeros((LANES - kn, LANES), F32)], axis=0)
        return rows.T.astype(BF16)

    idx_rows, gate_rows = [], []
    for e in range(ne):
        onehot = jnp.where(sub == blk[e:e + 1, :], 1.0, 0.0).astype(BF16)
        cnt = _dot(block_rows_t(cin_s, e), onehot)
        lane = jnp.sum(jnp.where(cnt <= rank[e:e + 1, :], 1.0, 0.0), axis=0, keepdims=True)
        idx_rows.append(blk[e:e + 1, :] * LANES + lane)
        val = (_dot(block_rows_t(parts_s.at[0], e), onehot)
               + _dot(block_rows_t(parts_s.at[1], e), onehot)
               + _dot(block_rows_t(parts_s.at[2], e), onehot))
        gate_rows.append(jnp.sum(jnp.where(sub == lane, val, 0.0), axis=0, keepdims=True))
    idx = jnp.concatenate(idx_rows, axis=0)[:, :cap].astype(jnp.int32) + tok_off
    return idx, jnp.concatenate(gate_rows, axis=0)[:, :cap]


def _topk_kernel(aff_ref, idx_ref, gate_ref, cin_s, parts_s, *, kl, kc, cap_l, cap_c):
    idx, gates = _select_stream(aff_ref[0:kl], cin_s, parts_s, cap_l, 0)
    idx_ref[:, 0:cap_l] = idx
    gate_ref[:, 0:cap_l] = gates
    if kc:
        idx, gates = _select_stream(aff_ref[kl:kl + kc], cin_s, parts_s, cap_c, kl * LANES)
        idx_ref[:, cap_l:cap_l + cap_c] = idx
        gate_ref[:, cap_l:cap_l + cap_c] = gates


def _topk(aff3, *, kl, kc, cap_l, cap_c):
    n_samples, _, e, _ = aff3.shape
    r = cap_l + (cap_c if kc else 0)
    kt = kl + kc
    return pl.pallas_call(
        functools.partial(_topk_kernel, kl=kl, kc=kc, cap_l=cap_l, cap_c=cap_c),
        grid=(n_samples,),
        in_specs=[pl.BlockSpec((None, kt, e, LANES), lambda b: (b, 0, 0, 0))],
        out_specs=[
            pl.BlockSpec((None, e, r), lambda b: (b, 0, 0)),
            pl.BlockSpec((None, e, r), lambda b: (b, 0, 0)),
        ],
        out_shape=[
            jax.ShapeDtypeStruct((n_samples, e, r), jnp.int32),
            jax.ShapeDtypeStruct((n_samples, e, r), F32),
        ],
        scratch_shapes=[pltpu.VMEM((kt * e, LANES), F32), pltpu.VMEM((3, kt * e, LANES), F32)],
        compiler_params=_cparams(("arbitrary",)),
        name="route_topk",
    )(aff3)


def _gather_kernel(idx_hbm, f_ref, o_ref, idx_s, stg, sem, *, r):
    b = pl.program_id(0)
    e = pl.program_id(1)

    @pl.when(e == 0)
    def _():
        cp = pltpu.make_async_copy(idx_hbm.at[b], idx_s, sem)
        cp.start()
        cp.wait()

    base = e * r

    def body(j, carry):
        n = idx_s[base + j]
        src = pl.multiple_of(n * SUBLANES, SUBLANES)
        dst = pl.multiple_of(j * SUBLANES, SUBLANES)
        stg[pl.ds(dst, SUBLANES), :] = f_ref[pl.ds(src, SUBLANES), :]
        return carry

    lax.fori_loop(0, r, body, 0, unroll=8)
    o_ref[...] = _load_rows(stg, r).astype(BF16)


def _gather(idx, fa, *, n_samples, nt, r):
    d = SUBLANES * LANES
    e = N_EXPERTS
    return pl.pallas_call(
        functools.partial(_gather_kernel, r=r),
        grid=(n_samples, e),
        in_specs=[
            pl.BlockSpec(memory_space=pl.ANY),
            pl.BlockSpec((nt * SUBLANES, LANES), lambda b, j: (b, 0)),
        ],
        out_specs=pl.BlockSpec((None, None, r, d), lambda b, j: (b, j, 0, 0)),
        out_shape=jax.ShapeDtypeStruct((n_samples, e, r, d), BF16),
        scratch_shapes=[
            pltpu.SMEM((idx.shape[1],), jnp.int32),
            pltpu.VMEM((r * SUBLANES, LANES), F32),
            pltpu.SemaphoreType.DMA,
        ],
        compiler_params=_cparams(("arbitrary", "arbitrary")),
        name="moe_gather",
    )(idx, fa)


def _expert_kernel(xg_ref, gate_ref, w1_ref, w3_ref, w2_ref, y_ref, *, r, fc):
    xg = xg_ref[...]
    f = w1_ref.shape[-1]
    acc = None
    for c in range(f // fc):
        sl = slice(c * fc, (c + 1) * fc)
        a = _dot(xg, w1_ref[:, sl])
        g = _dot(xg, w3_ref[:, sl])
        h = (a * jax.nn.sigmoid(a) * g).astype(BF16)
        part = _dot(h, w2_ref[sl, :])
        acc = part if acc is None else acc + part
    _store_rows(y_ref, acc * gate_ref[...], r)


def _experts(xg, gates, w1, w3, w2, layer):
    n_samples, e, r, d = xg.shape
    f = w1.shape[-1]
    return pl.pallas_call(
        functools.partial(_expert_kernel, r=r, fc=512),
        grid=(e, n_samples),
        in_specs=[
            pl.BlockSpec((None, None, r, d), lambda j, b: (b, j, 0, 0)),
            pl.BlockSpec((None, None, r, 1), lambda j, b: (b, j, 0, 0)),
            pl.BlockSpec((None, None, d, f), lambda j, b: (layer, j, 0, 0)),
            pl.BlockSpec((None, None, d, f), lambda j, b: (layer, j, 0, 0)),
            pl.BlockSpec((None, None, f, d), lambda j, b: (layer, j, 0, 0)),
        ],
        out_specs=pl.BlockSpec((None, None, r * SUBLANES, LANES), lambda j, b: (b, j, 0, 0)),
        out_shape=jax.ShapeDtypeStruct((n_samples, e, r * SUBLANES, LANES), F32),
        compiler_params=_cparams(("arbitrary", "arbitrary")),
        name="moe_experts",
    )(xg, gates, w1, w3, w2)


def _combine_kernel(idx_hbm, x_hbm, y_ref, g2b_ref, g2c_ref, o_hbm, idx_s, acc, sem, *,
                    r, rl, rows):
    b = pl.program_id(0)
    e = pl.program_id(1)
    n_exp = pl.num_programs(1)

    @pl.when(e == 0)
    def _():
        cp_i = pltpu.make_async_copy(idx_hbm.at[b], idx_s, sem.at[0])
        cp_x = pltpu.make_async_copy(x_hbm.at[pl.ds(b * rows, rows), :], acc, sem.at[1])
        cp_i.start()
        cp_x.start()
        cp_i.wait()
        cp_x.wait()

    base = e * r
    group = 8

    def add_rows(j0, gate):
        ns = [pl.multiple_of(idx_s[base + j0 + u] * SUBLANES, SUBLANES) for u in range(group)]
        vals = [acc[pl.ds(ns[u], SUBLANES), :]
                + y_ref[pl.ds(pl.multiple_of((j0 + u) * SUBLANES, SUBLANES), SUBLANES), :] * gate
                for u in range(group)]
        for u in range(group):
            acc[pl.ds(ns[u], SUBLANES), :] = vals[u]

    g2b = g2b_ref[...]

    def lat_body(jj, carry):
        add_rows(jj * group, g2b)
        return carry

    lax.fori_loop(0, rl // group, lat_body, 0)
    if r > rl:
        g2c = g2c_ref[...]

        def ctx_body(jj, carry):
            add_rows(rl + jj * group, g2c)
            return carry

        lax.fori_loop(0, (r - rl) // group, ctx_body, 0)

    @pl.when(e == n_exp - 1)
    def _():
        cp_o = pltpu.make_async_copy(acc, o_hbm.at[pl.ds(b * rows, rows), :], sem.at[2])
        cp_o.start()
        cp_o.wait()


def _combine(idx, xa, y, g2, *, n_samples, nt, r, rl):
    e = N_EXPERTS
    rows = nt * SUBLANES
    return pl.pallas_call(
        functools.partial(_combine_kernel, r=r, rl=rl, rows=rows),
        grid=(n_samples, e),
        in_specs=[
            pl.BlockSpec(memory_space=pl.ANY),
            pl.BlockSpec(memory_space=pl.ANY),
            pl.BlockSpec((None, None, r * SUBLANES, LANES), lambda b, j: (b, j, 0, 0)),
            pl.BlockSpec((None, SUBLANES, LANES), lambda b, j: (b, 0, 0)),
            pl.BlockSpec((None, SUBLANES, LANES), lambda b, j: (n_samples, 0, 0)),
        ],
        out_specs=pl.BlockSpec(memory_space=pl.ANY),
        out_shape=jax.ShapeDtypeStruct(xa.shape, F32),
        scratch_shapes=[
            pltpu.SMEM((idx.shape[1],), jnp.int32),
            pltpu.VMEM((rows, LANES), F32),
            pltpu.SemaphoreType.DMA((3,)),
        ],
        compiler_params=_cparams(("arbitrary", "arbitrary")),
        name="moe_combine",
    )(idx, xa, y, g2, g2)


def _final_kernel(x_ref, g_ref, o_ref):
    x = _load_rows(x_ref, TM)
    o_ref[...] = x * lax.rsqrt(jnp.mean(x * x, axis=-1, keepdims=True) + EPS) * g_ref[...]


def _final_norm(xa, g, *, n_samples, seq, tps):
    d = g.shape[-1]
    lat_tiles = seq // TM
    return pl.pallas_call(
        _final_kernel,
        grid=(n_samples * lat_tiles,),
        in_specs=[
            pl.BlockSpec((TM * SUBLANES, LANES),
                         lambda t: ((t // lat_tiles) * tps + t % lat_tiles, 0)),
            pl.BlockSpec((1, d), lambda t: (0, 0)),
        ],
        out_specs=pl.BlockSpec((TM, d), lambda t: (t, 0)),
        out_shape=jax.ShapeDtypeStruct((n_samples * seq, d), F32),
        compiler_params=_cparams(("arbitrary",)),
        name="final_norm",
    )(xa, g.reshape(1, d))


def kernel(x, c, ctx, c_ctx, w_mod, b_mod, g_norm_mix, g_norm_ffn, a_w_in, a_b_in, a_ln_g, a_ln_b,
           a_w_s, a_b_s, a_w_out, b_w_qkv, b_sink, b_w_o, r_w, e_w1, e_w3, e_w2, g_final):
    n_samples, seq, d = x.shape
    n_ctx = ctx.shape[1]
    depth = w_mod.shape[0]
    n_exp = r_w.shape[-1]
    assert d == SUBLANES * LANES and n_exp == N_EXPERTS
    assert seq % TM == 0 and n_ctx == TM and n_samples < MOD_ROWS
    nt = seq + n_ctx
    tps = nt // TM
    lat_tiles = seq // TM
    cap_l = (CAPACITY_FACTOR * seq) // n_exp
    cap_c = (CAPACITY_FACTOR * n_ctx) // n_exp

    xa = jnp.concatenate([x, ctx], axis=1).reshape(n_samples * nt * SUBLANES, LANES)

    rows = seq // GRID_W
    row = jnp.broadcast_to(jnp.arange(rows)[:, None], (rows, GRID_W)).reshape(-1).astype(F32)
    col = jnp.broadcast_to(jnp.arange(GRID_W)[None, :], (rows, GRID_W)).reshape(-1).astype(F32)
    pairs = HEAD_DIM // 4
    inv_freq = ROPE_THETA ** (-jnp.arange(pairs, dtype=F32) / pairs)
    ang = jnp.concatenate([row[:, None] * inv_freq, col[:, None] * inv_freq], axis=-1)
    cos = jnp.cos(ang)
    sin = jnp.sin(ang)
    cos_t = jnp.concatenate([jnp.tile(cos, (1, 4)), jnp.ones((n_ctx, LANES), F32)], axis=0)
    sin_t = jnp.concatenate([jnp.tile(jnp.concatenate([-sin, sin], axis=1), (1, 2)),
                             jnp.zeros((n_ctx, LANES), F32)], axis=0)

    cvec = jnp.zeros((MOD_ROWS, d), F32).at[:n_samples].set(c).at[n_samples].set(c_ctx)
    mod_all = _modulation(cvec, w_mod, b_mod).reshape(depth, MOD_ROWS, 6, d)
    mod_all = jnp.pad(mod_all, ((0, 0), (0, 0), (0, SUBLANES - 6), (0, 0)))

    w1 = e_w1.astype(BF16)
    w3 = e_w3.astype(BF16)
    w2 = e_w2.astype(BF16)

    for i in range(depth):
        last = i == depth - 1
        is_attn = i % 2 == 1
        j = i // 2
        mod = mod_all[i]
        w_r_t = r_w[i].T
        kw = dict(tps=tps, lat_tiles=lat_tiles, n_samples=n_samples)
        if is_attn:
            q_t, k4, v_t = _qkv(xa, mod, g_norm_mix[i], b_w_qkv[j], cos_t, sin_t, **kw)
            xa, fa, aff3 = _attention(xa, q_t, k4, v_t, mod, b_sink[j], b_w_o[j], g_norm_ffn[i],
                                      w_r_t, n_samples=n_samples, seq=seq, n_ctx=n_ctx,
                                      ctx_out=not last)
        else:
            xa, fa, aff3 = _gmlp(xa, mod, g_norm_mix[i], a_w_in[j], a_b_in[j], a_ln_g[j],
                                 a_ln_b[j], a_w_s[j], a_b_s[j], a_w_out[j], g_norm_ffn[i], w_r_t,
                                 **kw)

        idx, gates = _topk(aff3, kl=seq // LANES, kc=0 if last else n_ctx // LANES,
                           cap_l=cap_l, cap_c=cap_c)
        r = idx.shape[2]
        flat = idx.reshape(n_samples, n_exp * r)
        flat = jnp.pad(flat, ((0, 0), (0, -flat.shape[1] % 1024)))
        xg = _gather(flat, fa, n_samples=n_samples, nt=fa.shape[0] // (n_samples * SUBLANES), r=r)
        y = _experts(xg, gates[..., None], w1, w3, w2, i)
        g2 = mod[:, 5].reshape(MOD_ROWS, SUBLANES, LANES)
        xa = _combine(flat, xa, y, g2, n_samples=n_samples, nt=nt, r=r, rl=cap_l)

    out = _final_norm(xa, g_final, n_samples=n_samples, seq=seq, tps=tps)
    return out.reshape(n_samples, seq, d)
```

```python
import functools

import jax
import jax.numpy as jnp
from jax import lax
from jax.experimental import pallas as pl
from jax.experimental.pallas import tpu as pltpu

GRID_W = 64
CHUNK = 128
GA = 16
HEAD_DIM = 64
N_HEADS = 16
N_KV_HEADS = 4
Q_PER_KV = N_HEADS // N_KV_HEADS
WINDOW = 128
BLOCK = 128
ROPE_THETA = 10000.0
N_EXPERTS = 16
CAPACITY_FACTOR = 2
EPS = 1e-6

LANES = 128
SUBLANES = 8
TM = 256
MOD_ROWS = 24
ATTN_LOOKAHEAD = 3
GMLP_COLS = 512
GMLP_LOOKAHEAD = 2
VMEM_LIMIT = 56 * 1024 * 1024

BF16 = jnp.bfloat16
F32 = jnp.float32


def _cparams(sem, vmem=VMEM_LIMIT):
    return pltpu.CompilerParams(dimension_semantics=sem, vmem_limit_bytes=vmem)


def _load_rows(ref, rows):
    return jnp.concatenate(
        [ref[pl.ds(c, rows, stride=SUBLANES), :] for c in range(SUBLANES)], axis=1)


def _store_rows(ref, val, rows):
    for c in range(SUBLANES):
        ref[pl.ds(c, rows, stride=SUBLANES), :] = val[:, c * LANES:(c + 1) * LANES]


def _rms_mod(x, g, scale, shift):
    y = x * lax.rsqrt(jnp.mean(x * x, axis=-1, keepdims=True) + EPS)
    return (y * g) * (1.0 + scale) + shift


def _split(x):
    hi = x.astype(BF16)
    lo = (x - hi.astype(F32)).astype(BF16)
    return hi, lo


def _dot(a, b):
    return jnp.dot(a, b, preferred_element_type=F32)


def _dot_nt(a, b):
    return lax.dot_general(a, b, (((1,), (1,)), ((), ())), preferred_element_type=F32)


def _gelu(x):
    k = 0.7978845608028654
    inner = x * ((x * x) * (k * 0.044715) + k)
    hx = 0.5 * x
    return hx * jnp.tanh(inner) + hx


def _route_epilogue(xn, m, gnf_ref, wr_ref, f_ref, aff_ref, rows):
    f = _rms_mod(xn, gnf_ref[...], m[4:5], m[3:4])
    _store_rows(f_ref, f, rows)
    f_hi, f_lo = _split(f)
    w_hi, w_lo = _split(wr_ref[...])
    logits = _dot_nt(w_hi, f_hi) + _dot_nt(w_lo, f_hi) + _dot_nt(w_hi, f_lo)
    ex = jnp.exp(logits - jnp.max(logits, axis=0, keepdims=True))
    aff = ex / jnp.sum(ex, axis=0, keepdims=True)
    for kk in range(rows // LANES):
        aff_ref[kk] = aff[:, kk * LANES:(kk + 1) * LANES]


def _mod_kernel(c_ref, w_ref, b_ref, o_ref):
    c = c_ref[...]
    s = c * jax.nn.sigmoid(c)
    s_hi, s_lo = _split(s)
    w_hi, w_lo = _split(w_ref[...])
    o_ref[...] = _dot(s_hi, w_hi) + _dot(s_lo, w_hi) + _dot(s_hi, w_lo) + b_ref[...]


def _modulation(cvec, w_mod, b_mod):
    depth, d, d6 = w_mod.shape
    tn = 1024
    return pl.pallas_call(
        _mod_kernel,
        grid=(depth, d6 // tn),
        in_specs=[
            pl.BlockSpec((MOD_ROWS, d), lambda i, j: (0, 0)),
            pl.BlockSpec((None, d, tn), lambda i, j: (i, 0, j)),
            pl.BlockSpec((None, 1, tn), lambda i, j: (i, 0, j)),
        ],
        out_specs=pl.BlockSpec((None, MOD_ROWS, tn), lambda i, j: (i, 0, j)),
        out_shape=jax.ShapeDtypeStruct((depth, MOD_ROWS, d6), F32),
        compiler_params=_cparams(("arbitrary", "arbitrary")),
        name="modulation",
    )(cvec, w_mod, b_mod.reshape(depth, 1, d6))


def _gmlp_kernel(x_ref, mod_ref, gn_ref, win_ref, bin_ref, lng_ref, lnb_ref, ws_ref, bst_ref,
                 wout_ref, gnf_ref, wr_ref, o_ref, f_ref, aff_ref, *, da):
    x = _load_rows(x_ref, TM)
    m = mod_ref[...]
    h = _rms_mod(x, gn_ref[...], m[1:2], m[0:1]).astype(BF16)
    cc = GMLP_COLS
    n_half = da // cc

    def zchunk(j):
        return _dot(h, win_ref[:, j * cc:(j + 1) * cc]) + bin_ref[:, j * cc:(j + 1) * cc]

    order = list(range(n_half, 2 * n_half)) + list(range(n_half))
    pend = [zchunk(j) for j in order[:GMLP_LOOKAHEAD]]
    act = {}
    vn = None
    for t, j in enumerate(order):
        z = pend.pop(0)
        if t + GMLP_LOOKAHEAD < len(order):
            pend.append(zchunk(order[t + GMLP_LOOKAHEAD]))
        act[j] = _gelu(z)
        if t == n_half - 1:
            vs = [act[n_half + q] for q in range(n_half)]
            mu = sum(jnp.sum(a, axis=-1, keepdims=True) for a in vs) / da
            var = sum(jnp.sum(jnp.square(a - mu), axis=-1, keepdims=True) for a in vs) / da
            rs = lax.rsqrt(var + EPS)
            vn = [((a - mu) * rs * lng_ref[:, q * cc:(q + 1) * cc]
                   + lnb_ref[:, q * cc:(q + 1) * cc]).astype(BF16) for q, a in enumerate(vs)]
    cw = da // GA
    gpc = cc // cw

    def spatial(q):
        cols = []
        for gg in range(gpc):
            g = q * gpc + gg
            rows = [_dot(ws_ref[g], vn[q][r * CHUNK:(r + 1) * CHUNK, gg * cw:(gg + 1) * cw])
                    + bst_ref[:, g:g + 1] for r in range(TM // CHUNK)]
            cols.append(jnp.concatenate(rows, axis=0))
        return jnp.concatenate(cols, axis=1)

    o = None
    sv_next = spatial(0)
    for q in range(n_half):
        sv = sv_next
        if q + 1 < n_half:
            sv_next = spatial(q + 1)
        part = _dot((act[q] * sv).astype(BF16), wout_ref[q * cc:(q + 1) * cc, :])
        o = part if o is None else o + part
    xn = x + m[2:3] * o
    _store_rows(o_ref, xn, TM)
    _route_epilogue(xn, m, gnf_ref, wr_ref, f_ref, aff_ref, TM)


def _gmlp(xa, mod, gn, w_in, b_in, ln_g, ln_b, w_s, b_s, w_out, gnf, w_r_t, *, tps, lat_tiles,
          n_samples):
    d = gn.shape[-1]
    da = ln_g.shape[-1]
    n_tiles = xa.shape[0] // (TM * SUBLANES)
    bpt = TM // LANES

    def mod_idx(t):
        return (jnp.where(t % tps < lat_tiles, t // tps, n_samples), 0, 0)

    const2 = lambda t: (0, 0)
    tile = pl.BlockSpec((TM * SUBLANES, LANES), lambda t: (t, 0))
    return pl.pallas_call(
        functools.partial(_gmlp_kernel, da=da),
        grid=(n_tiles,),
        in_specs=[
            tile,
            pl.BlockSpec((None, SUBLANES, d), mod_idx),
            pl.BlockSpec((1, d), const2),
            pl.BlockSpec((d, 2 * da), const2),
            pl.BlockSpec((1, 2 * da), const2),
            pl.BlockSpec((1, da), const2),
            pl.BlockSpec((1, da), const2),
            pl.BlockSpec((GA, CHUNK, CHUNK), lambda t: (0, 0, 0)),
            pl.BlockSpec((CHUNK, GA), const2),
            pl.BlockSpec((da, d), const2),
            pl.BlockSpec((1, d), const2),
            pl.BlockSpec((N_EXPERTS, d), const2),
        ],
        out_specs=[
            tile,
            tile,
            pl.BlockSpec((None, bpt, N_EXPERTS, LANES), lambda t: (t // tps, t % tps, 0, 0)),
        ],
        out_shape=[
            jax.ShapeDtypeStruct(xa.shape, F32),
            jax.ShapeDtypeStruct(xa.shape, F32),
            jax.ShapeDtypeStruct((n_samples, tps * bpt, N_EXPERTS, LANES), F32),
        ],
        compiler_params=_cparams(("arbitrary",)),
        name="gmlp_mixer",
    )(xa, mod, gn.reshape(1, d), w_in.astype(BF16), b_in.reshape(1, -1), ln_g.reshape(1, -1),
      ln_b.reshape(1, -1), w_s.astype(BF16), b_s.T, w_out.astype(BF16), gnf.reshape(1, d), w_r_t)


def _qkv_kernel(x_ref, mod_ref, gn_ref, w_ref, cos_ref, sin_ref, q_ref, k_ref, v_ref, *, nq, nk):
    x = _load_rows(x_ref, TM)
    m = mod_ref[...]
    h = _rms_mod(x, gn_ref[...], m[1:2], m[0:1]).astype(BF16)
    qkv = _dot(h, w_ref[...])
    cs = cos_ref[...]
    sn = sin_ref[...]
    lane = lax.broadcasted_iota(jnp.int32, (TM, LANES), 1)
    first_half = (lane % HEAD_DIM) < (HEAD_DIM // 2)

    def rope(t):
        sw = jnp.where(first_half, pltpu.roll(t, LANES - HEAD_DIM // 2, 1),
                       pltpu.roll(t, HEAD_DIM // 2, 1))
        return t * cs + sw * sn

    scale = HEAD_DIM ** -0.5
    for j in range(nq // LANES):
        qj = rope(qkv[:, j * LANES:(j + 1) * LANES]) * scale
        q_ref[j * LANES:(j + 1) * LANES, :] = qj.T.astype(BF16)
    for j in range(nk // LANES):
        kj = rope(qkv[:, nq + j * LANES:nq + (j + 1) * LANES]).astype(BF16)
        for u in range(LANES // HEAD_DIM):
            k_ref[j * (LANES // HEAD_DIM) + u] = kj[:, u * HEAD_DIM:(u + 1) * HEAD_DIM]
        vj = qkv[:, nq + nk + j * LANES:nq + nk + (j + 1) * LANES]
        v_ref[j * LANES:(j + 1) * LANES, :] = vj.T.astype(BF16)


def _qkv(xa, mod, gn, w_qkv, cos_t, sin_t, *, tps, lat_tiles, n_samples):
    d = gn.shape[-1]
    nq = N_HEADS * HEAD_DIM
    nk = N_KV_HEADS * HEAD_DIM
    n_tok = xa.shape[0] // SUBLANES
    n_tiles = n_tok // TM

    def mod_idx(t):
        return (jnp.where(t % tps < lat_tiles, t // tps, n_samples), 0, 0)

    const2 = lambda t: (0, 0)
    return pl.pallas_call(
        functools.partial(_qkv_kernel, nq=nq, nk=nk),
        grid=(n_tiles,),
        in_specs=[
            pl.BlockSpec((TM * SUBLANES, LANES), lambda t: (t, 0)),
            pl.BlockSpec((None, SUBLANES, d), mod_idx),
            pl.BlockSpec((1, d), const2),
            pl.BlockSpec((d, nq + 2 * nk), const2),
            pl.BlockSpec((TM, LANES), lambda t: (t % tps, 0)),
            pl.BlockSpec((TM, LANES), lambda t: (t % tps, 0)),
        ],
        out_specs=[
            pl.BlockSpec((nq, TM), lambda t: (0, t)),
            pl.BlockSpec((N_KV_HEADS, TM, HEAD_DIM), lambda t: (0, t, 0)),
            pl.BlockSpec((nk, TM), lambda t: (0, t)),
        ],
        out_shape=[
            jax.ShapeDtypeStruct((nq, n_tok), BF16),
            jax.ShapeDtypeStruct((N_KV_HEADS, n_tok, HEAD_DIM), BF16),
            jax.ShapeDtypeStruct((nk, n_tok), BF16),
        ],
        compiler_params=_cparams(("arbitrary",)),
        name="qkv_rope",
    )(xa, mod, gn.reshape(1, d), w_qkv.astype(BF16), cos_t, sin_t)


def _attn_kernel(q_ref, k0_ref, k1_ref, k2_ref, kc_ref, v0_ref, v1_ref, v2_ref, vc_ref, sink_ref,
                 wo_ref, x_ref, mod_ref, gnf_ref, wr_ref, o_ref, f_ref, aff_ref, *, nb, seq):
    i = pl.program_id(1)
    nw = 3 * BLOCK
    jk = lax.broadcasted_iota(jnp.int32, (nw, BLOCK), 0)
    iq = lax.broadcasted_iota(jnp.int32, (nw, BLOCK), 1)
    pos = (i - 1) * BLOCK + jk
    mask = (jnp.abs(jk - BLOCK - iq) <= WINDOW) & (pos >= 0) & (pos < seq) & (i < nb)
    bias = jnp.where(mask, 0.0, -jnp.inf)
    vt = jnp.concatenate([v0_ref[...], v1_ref[...], v2_ref[...], vc_ref[...]], axis=1)
    kks = [jnp.concatenate([k0_ref[kh], k1_ref[kh], k2_ref[kh], kc_ref[kh]], axis=0)
           for kh in range(N_KV_HEADS)]

    def scores(h):
        return _dot(kks[h // Q_PER_KV], q_ref[h * HEAD_DIM:(h + 1) * HEAD_DIM, :])

    def softmax(h, s):
        s_w = s[:nw] + bias
        s_c = s[nw:]
        sk = sink_ref[h:h + 1, :]
        mx = jnp.maximum(jnp.maximum(jnp.max(s_w, axis=0, keepdims=True),
                                     jnp.max(s_c, axis=0, keepdims=True)), sk)
        p_w = jnp.exp(s_w - mx)
        p_c = jnp.exp(s_c - mx)
        den = (jnp.sum(p_w, axis=0, keepdims=True) + jnp.sum(p_c, axis=0, keepdims=True)
               + jnp.exp(sk - mx))
        return jnp.concatenate([p_w, p_c], axis=0).astype(BF16), den

    outs = []
    ss = [scores(h) for h in range(ATTN_LOOKAHEAD)]
    pend = None
    for h in range(N_HEADS):
        s_cur = ss.pop(0)
        if h + ATTN_LOOKAHEAD < N_HEADS:
            ss.append(scores(h + ATTN_LOOKAHEAD))
        p, den = softmax(h, s_cur)
        kh = h // Q_PER_KV
        o_raw = _dot(vt[kh * HEAD_DIM:(kh + 1) * HEAD_DIM], p)
        if pend is not None:
            outs.append(pend[0] / pend[1])
        pend = (o_raw, den)
    outs.append(pend[0] / pend[1])
    o_all = jnp.concatenate(outs, axis=0).T.astype(BF16)
    x = _load_rows(x_ref, BLOCK)
    m = mod_ref[...]
    xn = x + m[2:3] * _dot(o_all, wo_ref[...])
    _store_rows(o_ref, xn, BLOCK)
    _route_epilogue(xn, m, gnf_ref, wr_ref, f_ref, aff_ref, BLOCK)


def _attention(xa, q_t, k4, v_t, mod, sink, w_o, gnf, w_r_t, *, n_samples, seq, n_ctx, ctx_out):
    d = w_o.shape[-1]
    nq = N_HEADS * HEAD_DIM
    nk = N_KV_HEADS * HEAD_DIM
    nt = seq + n_ctx
    bps = nt // BLOCK
    nb = seq // BLOCK
    nqb = bps if ctx_out else nb

    def win(off):
        return lambda b, i: b * bps + jnp.clip(i + off, 0, nb - 1)

    ctx_blk = lambda b, i: (b * nt + seq) // n_ctx
    k_specs = [pl.BlockSpec((N_KV_HEADS, BLOCK, HEAD_DIM), lambda b, i, w=win(o): (0, w(b, i), 0))
               for o in (-1, 0, 1)]
    k_specs.append(pl.BlockSpec((N_KV_HEADS, n_ctx, HEAD_DIM), lambda b, i: (0, ctx_blk(b, i), 0)))
    v_specs = [pl.BlockSpec((nk, BLOCK), lambda b, i, w=win(o): (0, w(b, i))) for o in (-1, 0, 1)]
    v_specs.append(pl.BlockSpec((nk, n_ctx), lambda b, i: (0, ctx_blk(b, i))))
    row_idx = lambda b, i: (b * bps + i, 0)
    return pl.pallas_call(
        functools.partial(_attn_kernel, nb=nb, seq=seq),
        grid=(n_samples, nqb),
        in_specs=[pl.BlockSpec((nq, BLOCK), lambda b, i: (0, b * bps + i))] + k_specs + v_specs + [
            pl.BlockSpec((N_HEADS, LANES), lambda b, i: (0, 0)),
            pl.BlockSpec((nq, d), lambda b, i: (0, 0)),
            pl.BlockSpec((BLOCK * SUBLANES, LANES), row_idx),
            pl.BlockSpec((None, SUBLANES, d),
                         lambda b, i: (jnp.where(i < nb, b, n_samples), 0, 0)),
            pl.BlockSpec((1, d), lambda b, i: (0, 0)),
            pl.BlockSpec((N_EXPERTS, d), lambda b, i: (0, 0)),
        ],
        out_specs=[
            pl.BlockSpec((BLOCK * SUBLANES, LANES), row_idx),
            pl.BlockSpec((BLOCK * SUBLANES, LANES), lambda b, i: (b * nqb + i, 0)),
            pl.BlockSpec((None, 1, N_EXPERTS, LANES), lambda b, i: (b, i, 0, 0)),
        ],
        out_shape=[
            jax.ShapeDtypeStruct(xa.shape, F32),
            jax.ShapeDtypeStruct((n_samples * nqb * BLOCK * SUBLANES, LANES), F32),
            jax.ShapeDtypeStruct((n_samples, nqb, N_EXPERTS, LANES), F32),
        ],
        input_output_aliases={} if ctx_out else {11: 0},
        compiler_params=_cparams(("arbitrary", "arbitrary")),
        name="window_attention",
    )(q_t, k4, k4, k4, k4, v_t, v_t, v_t, v_t,
      jnp.broadcast_to(sink.astype(F32)[:, None], (N_HEADS, LANES)), w_o.astype(BF16), xa, mod,
      gnf.reshape(1, d), w_r_t)


def _select_stream(aff, cin_s, parts_s, cap, tok_off):
    kn, ne, _ = aff.shape
    n = kn * LANES
    capp = -(-cap // LANES) * LANES
    assert kn <= LANES
    bits = pltpu.bitcast(aff, jnp.int32)
    tok = (lax.broadcasted_iota(jnp.int32, aff.shape, 0) * LANES
           + lax.broadcasted_iota(jnp.int32, aff.shape, 2))

    def count(pred):
        return jnp.sum(jnp.sum(pred.astype(jnp.int32), axis=0), axis=1, keepdims=True)

    def value_step(s, thr):
        cand = thr | jnp.left_shift(jnp.int32(1), 30 - s)
        return jnp.where(count(bits >= cand) >= cap, cand, thr)

    thr = lax.fori_loop(0, 31, value_step, jnp.zeros((ne, 1), jnp.int32))
    above = bits > thr
    tied = bits == thr
    need = cap - count(above)

    def index_step(s, lim):
        cand = lim | jnp.left_shift(jnp.int32(1), n.bit_length() - 1 - s)
        return jnp.where(count(tied & (tok < cand)) < need, cand, lim)

    lim = lax.fori_loop(0, n.bit_length(), index_step, jnp.zeros((ne, 1), jnp.int32))
    chosen = above | (tied & (tok <= lim))

    flat = jnp.where(chosen, 1.0, 0.0).reshape(kn * ne, LANES).astype(BF16)
    li = lax.broadcasted_iota(jnp.int32, (LANES, LANES), 0)
    lj = lax.broadcasted_iota(jnp.int32, (LANES, LANES), 1)
    cin = _dot(flat, jnp.where(li <= lj, 1.0, 0.0).astype(BF16))
    cin_s[0:kn * ne, :] = cin
    a1 = aff.astype(BF16).astype(F32)
    a2 = (aff - a1).astype(BF16).astype(F32)
    for i, part in enumerate((a1, a2, aff - a1 - a2)):
        parts_s[i, 0:kn * ne, :] = part.reshape(kn * ne, LANES)

    jrow = lax.broadcasted_iota(jnp.int32, (ne, capp), 1).astype(F32)
    bend = jnp.zeros((ne, 1), F32)
    blk = jnp.zeros((ne, capp), F32)
    before = jnp.zeros((ne, capp), F32)
    for k in range(kn):
        btot = cin[k * ne:(k + 1) * ne, LANES - 1:LANES]
        bend = bend + btot
        full = jnp.where(bend <= jrow, 1.0, 0.0)
        blk = blk + full
        before = before + full * btot
    rank = jrow - before
    sub = lax.broadcasted_iota(jnp.int32, (LANES, capp), 0).astype(F32)

    def block_rows_t(ref, e):
        rows = ref[pl.ds(e, kn, stride=ne), :]
        if kn < LANES:
            rows = jnp.concatenate([rows, jnp.zeros((LANES - kn, LANES), F32)], axis=0)
        return rows.T.astype(BF16)

    idx_rows, gate_rows = [], []
    for e in range(ne):
        onehot = jnp.where(sub == blk[e:e + 1, :], 1.0, 0.0).astype(BF16)
        cnt = _dot(block_rows_t(cin_s, e), onehot)
        lane = jnp.sum(jnp.where(cnt <= rank[e:e + 1, :], 1.0, 0.0), axis=0, keepdims=True)
        idx_rows.append(blk[e:e + 1, :] * LANES + lane)
        val = (_dot(block_rows_t(parts_s.at[0], e), onehot)
               + _dot(block_rows_t(parts_s.at[1], e), onehot)
               + _dot(block_rows_t(parts_s.at[2], e), onehot))
        gate_rows.append(jnp.sum(jnp.where(sub == lane, val, 0.0), axis=0, keepdims=True))
    idx = jnp.concatenate(idx_rows, axis=0)[:, :cap].astype(jnp.int32) + tok_off
    return idx, jnp.concatenate(gate_rows, axis=0)[:, :cap]


def _topk_kernel(aff_ref, idx_ref, gate_ref, cin_s, parts_s, *, kl, kc, cap_l, cap_c):
    idx, gates = _select_stream(aff_ref[0:kl], cin_s, parts_s, cap_l, 0)
    idx_ref[:, 0:cap_l] = idx
    gate_ref[:, 0:cap_l] = gates
    if kc:
        idx, gates = _select_stream(aff_ref[kl:kl + kc], cin_s, parts_s, cap_c, kl * LANES)
        idx_ref[:, cap_l:cap_l + cap_c] = idx
        gate_ref[:, cap_l:cap_l + cap_c] = gates


def _topk(aff3, *, kl, kc, cap_l, cap_c):
    n_samples, _, e, _ = aff3.shape
    r = cap_l + (cap_c if kc else 0)
    kt = kl + kc
    return pl.pallas_call(
        functools.partial(_topk_kernel, kl=kl, kc=kc, cap_l=cap_l, cap_c=cap_c),
        grid=(n_samples,),
        in_specs=[pl.BlockSpec((None, kt, e, LANES), lambda b: (b, 0, 0, 0))],
        out_specs=[
            pl.BlockSpec((None, e, r), lambda b: (b, 0, 0)),
            pl.BlockSpec((None, e, r), lambda b: (b, 0, 0)),
        ],
        out_shape=[
            jax.ShapeDtypeStruct((n_samples, e, r), jnp.int32),
            jax.ShapeDtypeStruct((n_samples, e, r), F32),
        ],
        scratch_shapes=[pltpu.VMEM((kt * e, LANES), F32), pltpu.VMEM((3, kt * e, LANES), F32)],
        compiler_params=_cparams(("arbitrary",)),
        name="route_topk",
    )(aff3)


def _gather_kernel(idx_hbm, f_ref, o_ref, idx_s, stg, sem, *, r):
    b = pl.program_id(0)
    e = pl.program_id(1)

    @pl.when(e == 0)
    def _():
        cp = pltpu.make_async_copy(idx_hbm.at[b], idx_s, sem)
        cp.start()
        cp.wait()

    base = e * r

    def body(j, carry):
        n = idx_s[base + j]
        src = pl.multiple_of(n * SUBLANES, SUBLANES)
        dst = pl.multiple_of(j * SUBLANES, SUBLANES)
        stg[pl.ds(dst, SUBLANES), :] = f_ref[pl.ds(src, SUBLANES), :]
        return carry

    lax.fori_loop(0, r, body, 0, unroll=8)
    o_ref[...] = _load_rows(stg, r).astype(BF16)


def _gather(idx, fa, *, n_samples, nt, r):
    d = SUBLANES * LANES
    e = N_EXPERTS
    return pl.pallas_call(
        functools.partial(_gather_kernel, r=r),
        grid=(n_samples, e),
        in_specs=[
            pl.BlockSpec(memory_space=pl.ANY),
            pl.BlockSpec((nt * SUBLANES, LANES), lambda b, j: (b, 0)),
        ],
        out_specs=pl.BlockSpec((None, None, r, d), lambda b, j: (b, j, 0, 0)),
        out_shape=jax.ShapeDtypeStruct((n_samples, e, r, d), BF16),
        scratch_shapes=[
            pltpu.SMEM((idx.shape[1],), jnp.int32),
            pltpu.VMEM((r * SUBLANES, LANES), F32),
            pltpu.SemaphoreType.DMA,
        ],
        compiler_params=_cparams(("arbitrary", "arbitrary")),
        name="moe_gather",
    )(idx, fa)


def _expert_kernel(xg_ref, gate_ref, w1_ref, w3_ref, w2_ref, y_ref, *, r, fc):
    xg = xg_ref[...]
    f = w1_ref.shape[-1]
    acc = None
    for c in range(f // fc):
        sl = slice(c * fc, (c + 1) * fc)
        a = _dot(xg, w1_ref[:, sl].astype(BF16))
        g = _dot(xg, w3_ref[:, sl].astype(BF16))
        h = (a * jax.nn.sigmoid(a) * g).astype(BF16)
        part = _dot(h, w2_ref[sl, :].astype(BF16))
        acc = part if acc is None else acc + part
    _store_rows(y_ref, acc * gate_ref[...], r)


def _experts(xg, gates, w1, w3, w2, layer):
    n_samples, e, r, d = xg.shape
    f = w1.shape[-1]
    return pl.pallas_call(
        functools.partial(_expert_kernel, r=r, fc=512),
        grid=(e, n_samples),
        in_specs=[
            pl.BlockSpec((None, None, r, d), lambda j, b: (b, j, 0, 0)),
            pl.BlockSpec((None, None, r, 1), lambda j, b: (b, j, 0, 0)),
            pl.BlockSpec((None, None, d, f), lambda j, b: (layer, j, 0, 0),
                         pipeline_mode=pl.Buffered(1)),
            pl.BlockSpec((None, None, d, f), lambda j, b: (layer, j, 0, 0),
                         pipeline_mode=pl.Buffered(1)),
            pl.BlockSpec((None, None, f, d), lambda j, b: (layer, j, 0, 0),
                         pipeline_mode=pl.Buffered(1)),
        ],
        out_specs=pl.BlockSpec((None, None, r * SUBLANES, LANES), lambda j, b: (b, j, 0, 0)),
        out_shape=jax.ShapeDtypeStruct((n_samples, e, r * SUBLANES, LANES), F32),
        compiler_params=_cparams(("arbitrary", "arbitrary")),
        name="moe_experts",
    )(xg, gates, w1, w3, w2)


def _combine_kernel(idx_hbm, x_hbm, y_ref, g2b_ref, g2c_ref, o_hbm, idx_s, acc, sem, *,
                    r, rl, rows):
    b = pl.program_id(0)
    e = pl.program_id(1)
    n_exp = pl.num_programs(1)

    @pl.when(e == 0)
    def _():
        cp_i = pltpu.make_async_copy(idx_hbm.at[b], idx_s, sem.at[0])
        cp_x = pltpu.make_async_copy(x_hbm.at[pl.ds(b * rows, rows), :], acc, sem.at[1])
        cp_i.start()
        cp_x.start()
        cp_i.wait()
        cp_x.wait()

    base = e * r
    group = 8

    def add_rows(j0, gate):
        ns = [pl.multiple_of(idx_s[base + j0 + u] * SUBLANES, SUBLANES) for u in range(group)]
        vals = [acc[pl.ds(ns[u], SUBLANES), :]
                + y_ref[pl.ds(pl.multiple_of((j0 + u) * SUBLANES, SUBLANES), SUBLANES), :] * gate
                for u in range(group)]
        for u in range(group):
            acc[pl.ds(ns[u], SUBLANES), :] = vals[u]

    g2b = g2b_ref[...]

    def lat_body(jj, carry):
        add_rows(jj * group, g2b)
        return carry

    lax.fori_loop(0, rl // group, lat_body, 0)
    if r > rl:
        g2c = g2c_ref[...]

        def ctx_body(jj, carry):
            add_rows(rl + jj * group, g2c)
            return carry

        lax.fori_loop(0, (r - rl) // group, ctx_body, 0)

    @pl.when(e == n_exp - 1)
    def _():
        cp_o = pltpu.make_async_copy(acc, o_hbm.at[pl.ds(b * rows, rows), :], sem.at[2])
        cp_o.start()
        cp_o.wait()


def _combine(idx, xa, y, g2, *, n_samples, nt, r, rl):
    e = N_EXPERTS
    rows = nt * SUBLANES
    return pl.pallas_call(
        functools.partial(_combine_kernel, r=r, rl=rl, rows=rows),
        grid=(n_samples, e),
        in_specs=[
            pl.BlockSpec(memory_space=pl.ANY),
            pl.BlockSpec(memory_space=pl.ANY),
            pl.BlockSpec((None, None, r * SUBLANES, LANES), lambda b, j: (b, j, 0, 0)),
            pl.BlockSpec((None, SUBLANES, LANES), lambda b, j: (b, 0, 0)),
            pl.BlockSpec((None, SUBLANES, LANES), lambda b, j: (n_samples, 0, 0)),
        ],
        out_specs=pl.BlockSpec(memory_space=pl.ANY),
        out_shape=jax.ShapeDtypeStruct(xa.shape, F32),
        scratch_shapes=[
            pltpu.SMEM((idx.shape[1],), jnp.int32),
            pltpu.VMEM((rows, LANES), F32),
            pltpu.SemaphoreType.DMA((3,)),
        ],
        compiler_params=_cparams(("arbitrary", "arbitrary")),
        name="moe_combine",
    )(idx, xa, y, g2, g2)


def _final_kernel(x_ref, g_ref, o_ref):
    x = _load_rows(x_ref, TM)
    o_ref[...] = x * lax.rsqrt(jnp.mean(x * x, axis=-1, keepdims=True) + EPS) * g_ref[...]


def _final_norm(xa, g, *, n_samples, seq, tps):
    d = g.shape[-1]
    lat_tiles = seq // TM
    return pl.pallas_call(
        _final_kernel,
        grid=(n_samples * lat_tiles,),
        in_specs=[
            pl.BlockSpec((TM * SUBLANES, LANES),
                         lambda t: ((t // lat_tiles) * tps + t % lat_tiles, 0)),
            pl.BlockSpec((1, d), lambda t: (0, 0)),
        ],
        out_specs=pl.BlockSpec((TM, d), lambda t: (t, 0)),
        out_shape=jax.ShapeDtypeStruct((n_samples * seq, d), F32),
        compiler_params=_cparams(("arbitrary",)),
        name="final_norm",
    )(xa, g.reshape(1, d))


def kernel(x, c, ctx, c_ctx, w_mod, b_mod, g_norm_mix, g_norm_ffn, a_w_in, a_b_in, a_ln_g, a_ln_b,
           a_w_s, a_b_s, a_w_out, b_w_qkv, b_sink, b_w_o, r_w, e_w1, e_w3, e_w2, g_final):
    n_samples, seq, d = x.shape
    n_ctx = ctx.shape[1]
    depth = w_mod.shape[0]
    n_exp = r_w.shape[-1]
    assert d == SUBLANES * LANES and n_exp == N_EXPERTS
    assert seq % TM == 0 and n_ctx == TM and n_samples < MOD_ROWS
    nt = seq + n_ctx
    tps = nt // TM
    lat_tiles = seq // TM
    cap_l = (CAPACITY_FACTOR * seq) // n_exp
    cap_c = (CAPACITY_FACTOR * n_ctx) // n_exp

    xa = jnp.concatenate([x, ctx], axis=1).reshape(n_samples * nt * SUBLANES, LANES)

    rows = seq // GRID_W
    row = jnp.broadcast_to(jnp.arange(rows)[:, None], (rows, GRID_W)).reshape(-1).astype(F32)
    col = jnp.broadcast_to(jnp.arange(GRID_W)[None, :], (rows, GRID_W)).reshape(-1).astype(F32)
    pairs = HEAD_DIM // 4
    inv_freq = ROPE_THETA ** (-jnp.arange(pairs, dtype=F32) / pairs)
    ang = jnp.concatenate([row[:, None] * inv_freq, col[:, None] * inv_freq], axis=-1)
    cos = jnp.cos(ang)
    sin = jnp.sin(ang)
    cos_t = jnp.concatenate([jnp.tile(cos, (1, 4)), jnp.ones((n_ctx, LANES), F32)], axis=0)
    sin_t = jnp.concatenate([jnp.tile(jnp.concatenate([-sin, sin], axis=1), (1, 2)),
                             jnp.zeros((n_ctx, LANES), F32)], axis=0)

    cvec = jnp.zeros((MOD_ROWS, d), F32).at[:n_samples].set(c).at[n_samples].set(c_ctx)
    mod_all = _modulation(cvec, w_mod, b_mod).reshape(depth, MOD_ROWS, 6, d)
    mod_all = jnp.pad(mod_all, ((0, 0), (0, 0), (0, SUBLANES - 6), (0, 0)))

    w1, w3, w2 = e_w1, e_w3, e_w2

    for i in range(depth):
        last = i == depth - 1
        is_attn = i % 2 == 1
        j = i // 2
        mod = mod_all[i]
        w_r_t = r_w[i].T
        kw = dict(tps=tps, lat_tiles=lat_tiles, n_samples=n_samples)
        if is_attn:
            q_t, k4, v_t = _qkv(xa, mod, g_norm_mix[i], b_w_qkv[j], cos_t, sin_t, **kw)
            xa, fa, aff3 = _attention(xa, q_t, k4, v_t, mod, b_sink[j], b_w_o[j], g_norm_ffn[i],
                                      w_r_t, n_samples=n_samples, seq=seq, n_ctx=n_ctx,
                                      ctx_out=not last)
        else:
            xa, fa, aff3 = _gmlp(xa, mod, g_norm_mix[i], a_w_in[j], a_b_in[j], a_ln_g[j],
                                 a_ln_b[j], a_w_s[j], a_b_s[j], a_w_out[j], g_norm_ffn[i], w_r_t,
                                 **kw)

        idx, gates = _topk(aff3, kl=seq // LANES, kc=0 if last else n_ctx // LANES,
                           cap_l=cap_l, cap_c=cap_c)
        r = idx.shape[2]
        flat = idx.reshape(n_samples, n_exp * r)
        flat = jnp.pad(flat, ((0, 0), (0, -flat.shape[1] % 1024)))
        xg = _gather(flat, fa, n_samples=n_samples, nt=fa.shape[0] // (n_samples * SUBLANES), r=r)
        y = _experts(xg, gates[..., None], w1, w3, w2, i)
        g2 = mod[:, 5].reshape(MOD_ROWS, SUBLANES, LANES)
        xa = _combine(flat, xa, y, g2, n_samples=n_samples, nt=nt, r=r, rl=cap_l)

    out = _final_norm(xa, g_final, n_samples=n_samples, seq=seq, tps=tps)
    return out.reshape(n_samples, seq, d)
```
